```python
import math
import jax
import jax.numpy as jnp
from jax import lax
import numpy as np

D_MODEL = 4096
BATCH = 4
SEQ = 2048
DEPTH = 4
DEC_BATCH = 8
DEC_SEQ = 4
PAST_LEN = 8192
PAGE_SIZE = 128

N_MIXERS = 4
NORM_EPS = 1e-6
Q_BLOCK = 128
ROPE_THETA = 10000.0
RW_HEAD_DIM = 64
RW_HEADS = D_MODEL // RW_HEAD_DIM
RW_DECAY_LORA = max(32, int(round(1.8 * D_MODEL ** 0.5 / 32)) * 32)
RW_AAA_LORA = max(32, int(round(1.8 * D_MODEL ** 0.5 / 32)) * 32)
RW_GATE_LORA = max(32, int(round(0.6 * D_MODEL ** 0.8 / 32)) * 32)
RW_GN_EPS = 64e-5
SB_HEAD_DIM = 128
SB_HEADS = D_MODEL // SB_HEAD_DIM
DA_HEAD_DIM = 128
DA_HEADS = D_MODEL // (2 * DA_HEAD_DIM)
DA_SUBLN_EPS = 1e-5
GLA_HEADS = 4
GLA_KEY = D_MODEL // 2
GLA_DK = GLA_KEY // GLA_HEADS
GLA_DV = D_MODEL // GLA_HEADS
GLA_LORA = 16
GLA_TAU = 16.0
GLA_CHUNK = 32
GLA_NORM_EPS = 1e-5
MOE_GROUPS = 4
MOE_EXPERTS = 8
MOE_N_EXPERTS = MOE_GROUPS * MOE_EXPERTS
MOE_TOPK = 2
MOE_D_EXPERT = D_MODEL // 8
MOE_BLOCK = 128

kernel_name = 'hybrid_rwkv7_stickbreak_diffattn_gla_hmoe_adaln_step'

F32 = jnp.float32


def _rmsnorm(x, w, eps=NORM_EPS):
    xf = x.astype(F32)
    y = xf * lax.rsqrt(jnp.mean(xf * xf, axis=-1, keepdims=True) + eps)
    return (y * w.astype(F32)).astype(x.dtype)


def _adaln(c, w, b):
    mod = jax.nn.silu(c) @ w + b
    return jnp.split(mod[:, None, :], 6, axis=-1)


def _modulate(x, shift, scale):
    return x * (1.0 + scale) + shift


def _rope(x, pos):
    half = x.shape[-1] // 2
    inv_freq = ROPE_THETA ** (-jnp.arange(half, dtype=F32) / half)
    ang = pos.astype(F32)[:, None] * inv_freq[None, :]
    shape = (1, x.shape[1]) + (1,) * (x.ndim - 3) + (half,)
    cos = jnp.cos(ang).reshape(shape).astype(x.dtype)
    sin = jnp.sin(ang).reshape(shape).astype(x.dtype)
    x1, x2 = x[..., :half], x[..., half:]
    return jnp.concatenate([x1 * cos - x2 * sin, x2 * cos + x1 * sin], axis=-1)


def _gather_pages(cache, page_table):
    g = cache[page_table]
    return g.reshape((g.shape[0], g.shape[1] * g.shape[2]) + g.shape[3:])


def _sweep_query_blocks(block_fn, q_pos, *qs):
    t = q_pos.shape[0]
    blk = Q_BLOCK if t % Q_BLOCK == 0 else t
    n = t // blk

    def split(a):
        return jnp.moveaxis(a.reshape((a.shape[0], n, blk) + a.shape[2:]), 1, 0)

    out = lax.map(lambda args: block_fn(*args), (q_pos.reshape(n, blk),) + tuple(split(a) for a in qs))
    out = jnp.moveaxis(out, 0, 1)
    return out.reshape((out.shape[0], t) + out.shape[3:])


def _rwkv7_mix(h, shift_prev, wkv_prev, mu, w_r, w_k, w_v, w_o, w0, w1, w2, a0, a1, a2,
               g1, g2, k_k, k_a, r_k, ln_w, ln_b):
    b, t, d = h.shape
    x_prev = jnp.concatenate([shift_prev[:, None, :].astype(h.dtype), h[:, :-1]], axis=1)
    xx = x_prev - h
    xr, xw, xk, xv, xa, xg = (h + xx * mu[i] for i in range(6))
    r = xr @ w_r
    k = xk @ w_k
    v = xv @ w_v
    w_arg = (w0 + jnp.tanh(xw @ w1) @ w2).astype(F32)
    decay = jnp.exp(-jnp.exp(-jax.nn.softplus(-w_arg) - 0.5))
    a = jax.nn.sigmoid((a0 + (xa @ a1) @ a2).astype(F32))
    g = jax.nn.sigmoid(xg @ g1) @ g2

    def heads(z):
        return z.reshape(b, t, RW_HEADS, RW_HEAD_DIM).astype(F32)

    kk = heads(k * k_k)
    kk = kk / jnp.maximum(jnp.linalg.norm(kk, axis=-1, keepdims=True), 1e-12)
    k = k.astype(F32) * (1.0 + (a - 1.0) * k_a.astype(F32))
    rh, kh, vh, ah, wh = heads(r), heads(k), heads(v), heads(a), heads(decay)

    def step(s, inp):
        r_t, w_t, k_t, v_t, kk_t, a_t = inp
        sa = jnp.einsum('bhij,bhj->bhi', s, -kk_t)
        s = s * w_t[:, :, None, :] + sa[..., None] * (kk_t * a_t)[:, :, None, :] + v_t[..., None] * k_t[:, :, None, :]
        return s, jnp.einsum('bhij,bhj->bhi', s, r_t)

    tm = lambda z: jnp.moveaxis(z, 1, 0)
    s_fin, y = lax.scan(step, wkv_prev.astype(F32), (tm(rh), tm(wh), tm(kh), tm(vh), tm(kk), tm(ah)))
    y = jnp.moveaxis(y, 0, 1)
    mean = jnp.mean(y, axis=-1, keepdims=True)
    var = jnp.mean(jnp.square(y - mean), axis=-1, keepdims=True)
    y = ((y - mean) * lax.rsqrt(var + RW_GN_EPS)).reshape(b, t, d) * ln_w.astype(F32) + ln_b.astype(F32)
    bonus = jnp.sum(rh * kh * r_k.astype(F32), axis=-1, keepdims=True) * vh
    y = (y + bonus.reshape(b, t, d)) * g.astype(F32)
    return y.astype(h.dtype) @ w_o, h[:, -1], s_fin


def _stick_breaking_block(qp, q, k, v, k_pos):
    z = jnp.einsum('bqhd,bkhd->bhqk', q, k).astype(F32) * (SB_HEAD_DIM ** -0.5)
    before = k_pos[None, :] < qp[:, None]
    log_keep = jnp.where(before, jax.nn.log_sigmoid(-z), 0.0)
    later = lax.cumsum(log_keep, axis=3, reverse=True) - log_keep
    w = jnp.where(before, jnp.exp(jax.nn.log_sigmoid(z) + later), 0.0)
    return jnp.einsum('bhqk,bkhd->bqhd', w.astype(v.dtype), v)


def _sb_mix(h, pos, k_past, v_past, w_in, w_out):
    b, t, d = h.shape
    q, k, v = (z.reshape(b, t, SB_HEADS, SB_HEAD_DIM) for z in jnp.split(h @ w_in, 3, axis=-1))
    if k_past is None:
        k_all, v_all = k, v
    else:
        k_all = jnp.concatenate([k_past.astype(k.dtype), k], axis=1)
        v_all = jnp.concatenate([v_past.astype(v.dtype), v], axis=1)
    k_pos = jnp.arange(k_all.shape[1], dtype=jnp.int32)
    o = _sweep_query_blocks(lambda qp, qb: _stick_breaking_block(qp, qb, k_all, v_all, k_pos), pos, q)
    return o.reshape(b, t, d) @ w_out, k, v


def _diff_block(qp, q, k, v, k_pos, lam):
    s = jnp.einsum('bqhid,bkhid->bhiqk', q, k).astype(F32) * (DA_HEAD_DIM ** -0.5)
    causal = k_pos[None, :] <= qp[:, None]
    p = jax.nn.softmax(jnp.where(causal, s, -jnp.inf), axis=-1)
    w = p[:, :, 0] - lam * p[:, :, 1]
    return jnp.einsum('bhqk,bkhe->bqhe', w.astype(v.dtype), v)


def _diff_mix(h, pos, k_past, v_past, w_in, w_out, lam_qk, subln, lam_init):
    b, t, d = h.shape
    q, k, v = jnp.split(h @ w_in, 3, axis=-1)
    q = _rope(q.reshape(b, t, DA_HEADS, 2, DA_HEAD_DIM), pos)
    k = _rope(k.reshape(b, t, DA_HEADS, 2, DA_HEAD_DIM), pos)
    v = v.reshape(b, t, DA_HEADS, 2 * DA_HEAD_DIM)
    lf = lam_qk.astype(F32)
    lam = jnp.exp(jnp.sum(lf[0] * lf[1])) - jnp.exp(jnp.sum(lf[2] * lf[3])) + lam_init
    if k_past is None:
        k_all, v_all = k, v
    else:
        k_all = jnp.concatenate([k_past.astype(k.dtype), k], axis=1)
        v_all = jnp.concatenate([v_past.astype(v.dtype), v], axis=1)
    k_pos = jnp.arange(k_all.shape[1], dtype=jnp.int32)
    o = _sweep_query_blocks(lambda qp, qb: _diff_block(qp, qb, k_all, v_all, k_pos, lam), pos, q)
    o = _rmsnorm(o, subln, DA_SUBLN_EPS) * (1.0 - lam_init)
    return o.reshape(b, t, d) @ w_out, k, v


def _gla_chunked(q, k, v, log_a, s0):
    b, h, t, dk = q.shape
    dv = v.shape[-1]
    c = GLA_CHUNK if t % GLA_CHUNK == 0 else t
    n = t // c

    def chunks(z):
        return jnp.moveaxis(z.reshape(b, h, n, c, z.shape[-1]), 2, 0)

    lower = jnp.arange(c)[:, None] >= jnp.arange(c)[None, :]

    def step(s, inp):
        qc, kc, vc, gc = inp
        cum = jnp.cumsum(gc, axis=2)
        o_inter = jnp.einsum('bhtc,bhcv->bhtv', qc * jnp.exp(cum), s)
        rel = jnp.where(lower[None, None, :, :, None], cum[:, :, :, None, :] - cum[:, :, None, :, :], -jnp.inf)
        att = jnp.einsum('bhtc,bhsc,bhtsc->bhts', qc, kc, jnp.exp(rel))
        o = o_inter + jnp.einsum('bhts,bhsv->bhtv', att, vc)
        cum_end = cum[:, :, -1:, :]
        s = s * jnp.exp(cum_end)[:, :, 0, :, None] + jnp.einsum('bhsc,bhsv->bhcv', kc * jnp.exp(cum_end - cum), vc)
        return s, o

    s_fin, o = lax.scan(step, s0.astype(F32), (chunks(q), chunks(k), chunks(v), chunks(log_a)))
    o = jnp.moveaxis(o, 0, 2).reshape(b, h, t, dv)
    return o, s_fin


def _gla_mix(h, s0, w_in, wa1, wa2, ba, norm_w, w_out):
    b, t, d = h.shape
    q, k, v, gate = jnp.split(h @ w_in, [GLA_KEY, 2 * GLA_KEY, 2 * GLA_KEY + D_MODEL], axis=-1)
    log_a = jax.nn.log_sigmoid(((h @ wa1) @ wa2 + ba).astype(F32)) / GLA_TAU

    def hd(z, dh):
        return jnp.moveaxis(z.reshape(b, t, GLA_HEADS, dh), 2, 1).astype(F32)

    o, s_fin = _gla_chunked(hd(q, GLA_DK) * (GLA_DK ** -0.5), hd(k, GLA_DK), hd(v, GLA_DV), hd(log_a, GLA_DK), s0)
    o = _rmsnorm(jnp.moveaxis(o, 1, 2), norm_w, GLA_NORM_EPS)
    o = o.reshape(b, t, d) * jax.nn.silu(gate.astype(F32))
    return o.astype(h.dtype) @ w_out, s_fin


def _dispatch_experts(x, expert_id, gate, w_gate, w_up, w_down):
    n_tok, d = x.shape
    top_k = expert_id.shape[1]
    n_assign = n_tok * top_k
    n_exp = w_gate.shape[0]
    e_flat = expert_id.reshape(-1)
    order = jnp.argsort(e_flat)
    e_sorted = e_flat[order]
    tok_sorted = order // top_k
    gate_sorted = gate.reshape(-1)[order]
    counts = jnp.bincount(e_flat, length=n_exp)
    padded = (counts + MOE_BLOCK - 1) // MOE_BLOCK * MOE_BLOCK
    pad_end = jnp.cumsum(padded)
    rank = jnp.arange(n_assign) - (jnp.cumsum(counts) - counts)[e_sorted]
    dest = (pad_end - padded)[e_sorted] + rank
    n_blocks = -(-n_assign // MOE_BLOCK) + n_exp
    rows = jnp.zeros((n_blocks * MOE_BLOCK, d), x.dtype).at[dest].set(x[tok_sorted])
    block_expert = jnp.minimum(jnp.searchsorted(pad_end, jnp.arange(n_blocks) * MOE_BLOCK, side='right'), n_exp - 1)

    def expert_block(args):
        xb, e = args
        hb = jax.nn.silu(xb @ w_gate[e]) * (xb @ w_up[e])
        return hb @ w_down[e]

    out = lax.map(expert_block, (rows.reshape(n_blocks, MOE_BLOCK, d), block_expert))
    out = out.reshape(n_blocks * MOE_BLOCK, d)[dest]
    y = jax.ops.segment_sum(out.astype(F32) * gate_sorted[:, None], tok_sorted, num_segments=n_tok)
    return y.astype(x.dtype)


def _hier_moe(h, w_group, w_expert, w_gate, w_up, w_down):
    b, t, d = h.shape
    x = h.reshape(b * t, d)
    g_logits = (x @ w_group).astype(F32)
    grp = jnp.argmax(g_logits, axis=-1)
    g_w = jnp.take_along_axis(jax.nn.softmax(g_logits, axis=-1), grp[:, None], axis=-1)
    e_logits = jnp.einsum('nd,gde->nge', x, w_expert).astype(F32)
    e_logits = jnp.take_along_axis(e_logits, grp[:, None, None], axis=1)[:, 0]
    top_val, top_idx = lax.top_k(e_logits, MOE_TOPK)
    gate = g_w * jax.nn.softmax(top_val, axis=-1)
    expert_id = grp[:, None] * MOE_EXPERTS + top_idx
    return _dispatch_experts(x, expert_id, gate, w_gate, w_up, w_down).reshape(b, t, d)


def setup_inputs(seed: int = 0) -> dict:
    key = jax.random.key(seed)
    keys = iter(jax.random.split(key, 80))
    d = D_MODEL
    n_pages = PAST_LEN // PAGE_SIZE
    n_used = DEC_BATCH * n_pages
    n_phys = n_used + max(1, n_used // 4)

    def normal(shape, scale=1.0):
        return jax.random.normal(next(keys), shape, F32) * scale

    def gain(shape):
        return 1.0 + normal(shape, 0.05)

    page_table = jax.random.permutation(next(keys), n_phys)[:n_used].reshape(DEC_BATCH, n_pages).astype(jnp.int32)
    return {
        'x_prompt': normal((BATCH, SEQ, d)),
        'x_sample': normal((DEC_BATCH, DEC_SEQ, d)),
        'c_prompt': normal((BATCH, d)),
        'c_sample': normal((DEC_BATCH, d)),
        'state_rwkv_shift': normal((DEC_BATCH, d)),
        'state_rwkv_wkv': normal((DEC_BATCH, RW_HEADS, RW_HEAD_DIM, RW_HEAD_DIM), 0.1),
        'cache_sb_k': normal((n_phys, PAGE_SIZE, SB_HEADS, SB_HEAD_DIM)),
        'cache_sb_v': normal((n_phys, PAGE_SIZE, SB_HEADS, SB_HEAD_DIM)),
        'cache_da_k': normal((n_phys, PAGE_SIZE, DA_HEADS, 2, DA_HEAD_DIM)),
        'cache_da_v': normal((n_phys, PAGE_SIZE, DA_HEADS, 2 * DA_HEAD_DIM)),
        'state_gla': normal((DEC_BATCH, GLA_HEADS, GLA_DK, GLA_DV), 0.05),
        'page_table': page_table,
        'ada_w': normal((DEPTH, d, 6 * d), 0.5 * d ** -0.5),
        'ada_b': normal((DEPTH, 6 * d), 0.02),
        'norm_mix': gain((DEPTH, d)),
        'norm_ffn': gain((DEPTH, d)),
        'norm_final': gain((d,)),
        'rw_mu': jax.random.uniform(next(keys), (6, d), F32),
        'rw_w_r': normal((d, d), d ** -0.5),
        'rw_w_k': normal((d, d), d ** -0.5),
        'rw_w_v': normal((d, d), d ** -0.5),
        'rw_w_o': normal((d, d), d ** -0.5),
        'rw_w0': normal((d,), 0.5),
        'rw_w1': normal((d, RW_DECAY_LORA), d ** -0.5),
        'rw_w2': normal((RW_DECAY_LORA, d), RW_DECAY_LORA ** -0.5),
        'rw_a0': normal((d,), 0.1),
        'rw_a1': normal((d, RW_AAA_LORA), d ** -0.5),
        'rw_a2': normal((RW_AAA_LORA, d), RW_AAA_LORA ** -0.5),
        'rw_g1': normal((d, RW_GATE_LORA), d ** -0.5),
        'rw_g2': normal((RW_GATE_LORA, d), RW_GATE_LORA ** -0.5),
        'rw_k_k': 0.85 + normal((d,), 0.05),
        'rw_k_a': gain((d,)),
        'rw_r_k': normal((RW_HEADS, RW_HEAD_DIM), 0.1),
        'rw_ln_w': gain((d,)),
        'rw_ln_b': normal((d,), 0.02),
        'sb_w_in': normal((d, 3 * d), d ** -0.5),
        'sb_w_out': normal((d, d), d ** -0.5),
        'da_w_in': normal((d, 3 * d), d ** -0.5),
        'da_w_out': normal((d, d), d ** -0.5),
        'da_lambda': normal((4, DA_HEAD_DIM), 0.1),
        'da_subln': gain((2 * DA_HEAD_DIM,)),
        'gla_w_in': normal((d, 2 * GLA_KEY + 2 * d), d ** -0.5),
        'gla_wa1': normal((d, GLA_LORA), d ** -0.5),
        'gla_wa2': normal((GLA_LORA, GLA_KEY), GLA_LORA ** -0.5),
        'gla_ba': normal((GLA_KEY,), 0.1),
        'gla_norm': gain((GLA_DV,)),
        'gla_w_out': normal((d, d), d ** -0.5),
        'moe_w_group': normal((DEPTH, d, MOE_GROUPS), d ** -0.5),
        'moe_w_expert': normal((DEPTH, MOE_GROUPS, d, MOE_EXPERTS), d ** -0.5),
        'moe_w_gate': normal((DEPTH, MOE_N_EXPERTS, d, MOE_D_EXPERT), d ** -0.5),
        'moe_w_up': normal((DEPTH, MOE_N_EXPERTS, d, MOE_D_EXPERT), d ** -0.5),
        'moe_w_down': normal((DEPTH, MOE_N_EXPERTS, MOE_D_EXPERT, d), MOE_D_EXPERT ** -0.5),
    }


def reference(x_prompt, x_sample, c_prompt, c_sample, state_rwkv_shift, state_rwkv_wkv,
              cache_sb_k, cache_sb_v, cache_da_k, cache_da_v, state_gla, page_table,
              ada_w, ada_b, norm_mix, norm_ffn, norm_final,
              rw_mu, rw_w_r, rw_w_k, rw_w_v, rw_w_o, rw_w0, rw_w1, rw_w2, rw_a0, rw_a1, rw_a2,
              rw_g1, rw_g2, rw_k_k, rw_k_a, rw_r_k, rw_ln_w, rw_ln_b,
              sb_w_in, sb_w_out, da_w_in, da_w_out, da_lambda, da_subln,
              gla_w_in, gla_wa1, gla_wa2, gla_ba, gla_norm, gla_w_out,
              moe_w_group, moe_w_expert, moe_w_gate, moe_w_up, moe_w_down):
    n_p, t_p, d = x_prompt.shape
    t_s = x_sample.shape[1]
    past_len = page_table.shape[1] * PAGE_SIZE
    pos_p = jnp.arange(t_p, dtype=jnp.int32)
    pos_s = past_len + jnp.arange(t_s, dtype=jnp.int32)
    rw = (rw_mu, rw_w_r, rw_w_k, rw_w_v, rw_w_o, rw_w0, rw_w1, rw_w2, rw_a0, rw_a1, rw_a2,
          rw_g1, rw_g2, rw_k_k, rw_k_a, rw_r_k, rw_ln_w, rw_ln_b)
    xp, xs = x_prompt, x_sample
    for i in range(DEPTH):
        mp = _adaln(c_prompt, ada_w[i], ada_b[i])
        ms = _adaln(c_sample, ada_w[i], ada_b[i])
        hp = _modulate(_rmsnorm(xp, norm_mix[i]), mp[0], mp[1])
        hs = _modulate(_rmsnorm(xs, norm_mix[i]), ms[0], ms[1])
        kind = i % N_MIXERS
        if kind == 0:
            yp, rw_shift_p, rw_wkv_p = _rwkv7_mix(
                hp, jnp.zeros((n_p, d), hp.dtype), jnp.zeros((n_p, RW_HEADS, RW_HEAD_DIM, RW_HEAD_DIM), F32), *rw)
            ys, rw_shift_s, rw_wkv_s = _rwkv7_mix(hs, state_rwkv_shift, state_rwkv_wkv, *rw)
        elif kind == 1:
            yp, sb_k_p, sb_v_p = _sb_mix(hp, pos_p, None, None, sb_w_in, sb_w_out)
            ys, sb_k_s, sb_v_s = _sb_mix(hs, pos_s, _gather_pages(cache_sb_k, page_table),
                                         _gather_pages(cache_sb_v, page_table), sb_w_in, sb_w_out)
        elif kind == 2:
            lam_init = 0.8 - 0.6 * math.exp(-0.3 * i)
            yp, da_k_p, da_v_p = _diff_mix(hp, pos_p, None, None, da_w_in, da_w_out, da_lambda, da_subln, lam_init)
            ys, da_k_s, da_v_s = _diff_mix(hs, pos_s, _gather_pages(cache_da_k, page_table),
                                           _gather_pages(cache_da_v, page_table), da_w_in, da_w_out,
                                           da_lambda, da_subln, lam_init)
        else:
            yp, gla_p = _gla_mix(hp, jnp.zeros((n_p, GLA_HEADS, GLA_DK, GLA_DV), F32),
                                 gla_w_in, gla_wa1, gla_wa2, gla_ba, gla_norm, gla_w_out)
            ys, gla_s = _gla_mix(hs, state_gla, gla_w_in, gla_wa1, gla_wa2, gla_ba, gla_norm, gla_w_out)
        xp = xp + mp[2] * yp
        xs = xs + ms[2] * ys
        moe = (moe_w_group[i], moe_w_expert[i], moe_w_gate[i], moe_w_up[i], moe_w_down[i])
        xp = xp + mp[5] * _hier_moe(_modulate(_rmsnorm(xp, norm_ffn[i]), mp[3], mp[4]), *moe)
        xs = xs + ms[5] * _hier_moe(_modulate(_rmsnorm(xs, norm_ffn[i]), ms[3], ms[4]), *moe)
    y_prompt = _rmsnorm(xp, norm_final)
    y_sample = _rmsnorm(xs, norm_final)
    return (y_prompt, y_sample,
            rw_shift_p, rw_wkv_p, sb_k_p, sb_v_p, da_k_p, da_v_p, gla_p,
            rw_shift_s, rw_wkv_s, sb_k_s, sb_v_s, da_k_s, da_v_s, gla_s)
```

```python
import functools
import math

import jax
import jax.numpy as jnp
from jax import lax
from jax.experimental import pallas as pl
from jax.experimental.pallas import tpu as pltpu

F32 = jnp.float32
BF16 = jnp.bfloat16

LANES = 128
SUBLANES = 8
VMEM_LIMIT_BYTES = 56 * 1024 * 1024

NORM_EPS = 1e-6
ROPE_THETA = 10000.0
RW_HEAD_DIM = 64
RW_GN_EPS = 64e-5
SB_HEAD_DIM = 128
DA_HEAD_DIM = 128
DA_SUBLN_EPS = 1e-5
GLA_HEADS = 4
GLA_TAU = 16.0
GLA_NORM_EPS = 1e-5
MOE_TOPK = 2
SAMPLE_SLOT = 8


def _params(*semantics):
    return pltpu.CompilerParams(dimension_semantics=semantics, vmem_limit_bytes=VMEM_LIMIT_BYTES)


def _split3(x):
    hi = x.astype(BF16)
    r1 = x - hi.astype(F32)
    mid = r1.astype(BF16)
    lo = (r1 - mid.astype(F32)).astype(BF16)
    return hi, mid, lo


_NN = (((1,), (0,)), ((), ()))
_NT = (((1,), (1,)), ((), ()))
_TN = (((0,), (0,)), ((), ()))


def _dot1(a, b, dims=_NN):
    return lax.dot_general(a.astype(BF16), b.astype(BF16), dims, preferred_element_type=F32)


def _dot_exact_rhs(a, b_exact, dims=_NN):
    hi, mid, lo = _split3(a)
    b = b_exact.astype(BF16)
    return (lax.dot_general(hi, b, dims, preferred_element_type=F32)
            + lax.dot_general(mid, b, dims, preferred_element_type=F32)
            + lax.dot_general(lo, b, dims, preferred_element_type=F32))


def _dot_exact_lhs(a_exact, b, dims=_NN):
    hi, mid, lo = _split3(b)
    a = a_exact.astype(BF16)
    return (lax.dot_general(a, hi, dims, preferred_element_type=F32)
            + lax.dot_general(a, mid, dims, preferred_element_type=F32)
            + lax.dot_general(a, lo, dims, preferred_element_type=F32))


def _dot3(a, b, dims=_NN):
    ah, am, _ = _split3(a)
    bh, bm, _ = _split3(b)
    d = functools.partial(lax.dot_general, dimension_numbers=dims, preferred_element_type=F32)
    return d(ah, bh) + (d(ah, bm) + d(am, bh))


def _softplus(x):
    return jnp.maximum(x, 0.0) + jnp.log1p(jnp.exp(-jnp.abs(x)))


def _sigmoid(x):
    return 1.0 / (1.0 + jnp.exp(-x))


def _adaln_kernel(c_ref, w_ref, b_ref, o_ref):
    c = c_ref[...]
    s = c * _sigmoid(c)
    o_ref[...] = _dot3(s, w_ref[...]) + b_ref[...]


def _adaln(c_all, ada_w, ada_b):
    n_layers, d, n = ada_w.shape
    rows = c_all.shape[0]
    tn = min(512, n)
    return pl.pallas_call(
        _adaln_kernel,
        out_shape=jax.ShapeDtypeStruct((n_layers, rows, n), F32),
        grid=(n_layers, n // tn),
        in_specs=[
            pl.BlockSpec((rows, d), lambda l, j: (0, 0)),
            pl.BlockSpec((None, d, tn), lambda l, j: (l, 0, j)),
            pl.BlockSpec((None, 1, tn), lambda l, j: (l, 0, j)),
        ],
        out_specs=pl.BlockSpec((None, rows, tn), lambda l, j: (l, 0, j)),
        compiler_params=_params("parallel", "parallel"),
        name="adaln",
    )(c_all, ada_w, ada_b.reshape(n_layers, 1, n))


def _norm_mod_kernel(*refs, eps, modulate, n_out, router):
    x_ref, w_ref = refs[:2]
    pos = 2
    x = x_ref[...]
    y = x * lax.rsqrt(jnp.mean(x * x, axis=-1, keepdims=True) + eps) * w_ref[...]
    if modulate:
        shift_ref, scale_ref = refs[pos:pos + 2]
        pos += 2
        y = y * (1.0 + scale_ref[...]) + shift_ref[...]
    if router:
        wr_ref = refs[pos]
        pos += 1
    outs = refs[pos:]
    for o in outs[:n_out]:
        o[...] = y.astype(o.dtype)
    if router:
        outs[n_out][...] = _dot3(y, wr_ref[...])


def _mod_spec(mod, tm, tg, width):
    g, gr, _ = mod.shape
    if gr == 1:
        per = tg // tm
        return pl.BlockSpec((None, 1, width), lambda i, *_: (i // per, 0, 0))
    assert g == 1 and gr == tm
    return pl.BlockSpec((None, gr, width), lambda i, *_: (0, 0, 0))


def _norm_mod(x, w, shift, scale, *, tg, out_dtypes, eps=NORM_EPS, router_w=None):
    rows, d = x.shape
    tm = min(256, tg) if shift is None or shift.shape[1] == 1 else rows
    in_specs = [pl.BlockSpec((tm, d), lambda i: (i, 0)), pl.BlockSpec((1, d), lambda i: (0, 0))]
    args = [x, w.reshape(1, d)]
    if shift is not None:
        in_specs += [_mod_spec(shift, tm, tg, d), _mod_spec(scale, tm, tg, d)]
        args += [shift, scale]
    out_shape = [jax.ShapeDtypeStruct((rows, d), dt) for dt in out_dtypes]
    out_specs = [pl.BlockSpec((tm, d), lambda i: (i, 0)) for _ in out_dtypes]
    if router_w is not None:
        nr = router_w.shape[1]
        in_specs.append(pl.BlockSpec((d, nr), lambda i: (0, 0)))
        args.append(router_w)
        out_shape.append(jax.ShapeDtypeStruct((rows, nr), F32))
        out_specs.append(pl.BlockSpec((tm, nr), lambda i: (i, 0)))
    return pl.pallas_call(
        functools.partial(_norm_mod_kernel, eps=eps, modulate=shift is not None,
                          n_out=len(out_dtypes), router=router_w is not None),
        out_shape=out_shape,
        grid=(rows // tm,),
        in_specs=in_specs,
        out_specs=out_specs,
        compiler_params=_params("parallel"),
        name="norm_mod",
    )(*args)


def _matmul_kernel(*refs, act, resid):
    a_ref, b_ref = refs[:2]
    o_ref = refs[-1]
    acc = jnp.dot(a_ref[...], b_ref[...], preferred_element_type=F32)
    if act == "tanh":
        acc = jnp.tanh(acc)
    elif act == "sigmoid":
        acc = _sigmoid(acc)
    if resid:
        x_ref, g_ref = refs[2:4]
        acc = x_ref[...] + g_ref[...] * acc
    o_ref[...] = acc.astype(o_ref.dtype)


def _matmul(a, b, *, out_dtype=F32, act=None, resid=None, gate=None, tg=None):
    m, k = a.shape
    n = b.shape[1]
    tm = min(1024, m)
    if gate is not None:
        tm = min(tm, tg) if gate.shape[1] == 1 else m
    tn = min(512, n)
    in_specs = [pl.BlockSpec((tm, k), lambda i, j: (i, 0)), pl.BlockSpec((k, tn), lambda i, j: (0, j))]
    args = [a, b]
    if resid is not None:
        g, gr, _ = gate.shape
        in_specs.append(pl.BlockSpec((tm, tn), lambda i, j: (i, j)))
        if gr == 1:
            per = tg // tm
            in_specs.append(pl.BlockSpec((None, 1, tn), lambda i, j: (i // per, 0, j)))
        else:
            in_specs.append(pl.BlockSpec((None, gr, tn), lambda i, j: (0, 0, j)))
        args += [resid, gate]
    return pl.pallas_call(
        functools.partial(_matmul_kernel, act=act, resid=resid is not None),
        out_shape=jax.ShapeDtypeStruct((m, n), out_dtype),
        grid=(m // tm, n // tn),
        in_specs=in_specs,
        out_specs=pl.BlockSpec((tm, tn), lambda i, j: (i, j)),
        compiler_params=_params("parallel", "parallel"),
        name="matmul",
    )(*args)


def _rw_mix_kernel(h_ref, hp_ref, mu_ref, *o_refs):
    h = h_ref[...]
    xx = hp_ref[...] - h
    for i, o in enumerate(o_refs):
        o[...] = (h + xx * mu_ref[i:i + 1, :]).astype(o.dtype)


def _rw_mix(h, h_prev, mu):
    rows, d = h.shape
    tm = min(256, rows)
    n = mu.shape[0]
    spec = pl.BlockSpec((tm, d), lambda i: (i, 0))
    return pl.pallas_call(
        _rw_mix_kernel,
        out_shape=[jax.ShapeDtypeStruct((rows, d), BF16)] * n,
        grid=(rows // tm,),
        in_specs=[spec, spec, pl.BlockSpec((n, d), lambda i: (0, 0))],
        out_specs=[spec] * n,
        compiler_params=_params("parallel"),
        name="rw_mix",
    )(h, h_prev, mu)


def _rw_recur_kernel(r_ref, k_ref, v_ref, wl_ref, al_ref, g_ref, par_ref, s0_ref,
                     y_ref, sout_ref, s_scr, *, chunk, t_valid):
    c = chunk
    ci = pl.program_id(2)
    hd = RW_HEAD_DIM

    @pl.when(ci == 0)
    def _():
        s_scr[...] = s0_ref[...]

    par = par_ref[...]
    w0, a0, k_k, k_a, r_k, ln_w, ln_b = (par[i:i + 1, :] for i in range(7))
    r = r_ref[...]
    k = k_ref[...]
    v = v_ref[...]
    lane = lax.broadcasted_iota(jnp.int32, (1, LANES), 1)
    head0 = lane < hd
    lane_r = lax.broadcasted_iota(jnp.int32, (LANES, LANES), 0)
    lane_c = lax.broadcasted_iota(jnp.int32, (LANES, LANES), 1)
    same_head = (lane_r < hd) == (lane_c < hd)
    seg = jnp.where(same_head, 1.0, 0.0)
    eye = jnp.where(lane_r == lane_c, 1.0, 0.0)

    w_arg = w0 + wl_ref[...]
    lw = -jnp.exp(-_softplus(-w_arg) - 0.5)
    a = _sigmoid(a0 + al_ref[...])
    kk = k * k_k
    nrm = jnp.sqrt(_dot_exact_rhs(kk * kk, seg))
    kk = kk / jnp.maximum(nrm, 1e-12)
    kmod = k * (1.0 + (a - 1.0) * k_a)
    al = -kk
    be = kk * a
    if t_valid < c:
        valid = (ci * c + lax.broadcasted_iota(jnp.int32, (c, 1), 0)) < t_valid
        lw = jnp.where(valid, lw, 0.0)
        al = jnp.where(valid, al, 0.0)
        be = jnp.where(valid, be, 0.0)
        kmod_s = jnp.where(valid, kmod, 0.0)
    else:
        kmod_s = kmod

    ti = lax.broadcasted_iota(jnp.int32, (c, c), 0)
    si = lax.broadcasted_iota(jnp.int32, (c, c), 1)
    low_incl = ti >= si
    low_strict = ti > si
    cum = _dot_exact_lhs(jnp.where(low_incl, 1.0, 0.0), lw)
    cum_end = cum[c - 1:c, :]
    p_q = jnp.exp(cum)
    p_m = jnp.exp(cum - lw)
    p_i = jnp.exp(-cum)
    p_e = jnp.exp(cum_end - cum)
    a_t = al * p_m
    b_t = be * p_i
    k_t = kmod_s * p_i
    r_t = r * p_q
    b_bar = be * p_e
    k_bar = kmod_s * p_e

    eye_c = jnp.where(ti == si, 1.0, 0.0)
    a_hat = jnp.zeros((c, LANES), F32)
    u_hat = jnp.zeros((c, LANES), F32)
    r_hat = jnp.zeros((c, LANES), F32)
    y_hat = jnp.zeros((c, LANES), F32)
    per_head = []
    for hm in (head0, jnp.logical_not(head0)):
        a_m = jnp.where(hm, a_t, 0.0)
        r_m = jnp.where(hm, r_t, 0.0)
        l_ab = jnp.where(low_strict, _dot3(a_m, b_t, _NT), 0.0)
        l_ak = jnp.where(low_strict, _dot3(a_m, k_t, _NT), 0.0)
        l_rb = jnp.where(low_incl, _dot3(r_m, b_t, _NT), 0.0)
        l_rk = jnp.where(low_incl, _dot3(r_m, k_t, _NT), 0.0)
        t_inv = eye_c + l_ab
        pw = l_ab
        span = 2
        while span < c:
            pw = _dot3(pw, pw)
            t_inv = t_inv + _dot3(t_inv, pw)
            span *= 2
        a_hat = a_hat + jnp.where(hm, _dot3(t_inv, a_t), 0.0)
        u_hat = u_hat + jnp.where(hm, _dot3(t_inv, _dot3(l_ak, v)), 0.0)
        per_head.append((hm, l_rb, l_rk))
    for hm, l_rb, l_rk in per_head:
        r_hat = r_hat + jnp.where(hm, r_t + _dot3(l_rb, a_hat), 0.0)
        y_hat = y_hat + jnp.where(hm, _dot3(l_rb, u_hat) + _dot3(l_rk, v), 0.0)

    s = s_scr[...]
    y = _dot3(r_hat, s) + y_hat
    g_mat = eye * jnp.exp(cum_end) + jnp.where(same_head, _dot3(b_bar, a_hat, _TN), 0.0)
    z_mat = jnp.where(same_head, _dot3(b_bar, u_hat, _TN) + _dot3(k_bar, v, _TN), 0.0)
    s_new = _dot3(g_mat, s) + z_mat
    s_scr[...] = s_new

    @pl.when(ci == pl.num_programs(2) - 1)
    def _():
        sout_ref[...] = s_new

    inv_hd = 1.0 / hd
    mean = _dot_exact_rhs(y, seg) * inv_hd
    yc = y - mean
    var = _dot_exact_rhs(yc * yc, seg) * inv_hd
    yn = yc * lax.rsqrt(var + RW_GN_EPS) * ln_w + ln_b
    bonus = _dot_exact_rhs(r * kmod * r_k, seg) * v
    y_ref[...] = ((yn + bonus) * g_ref[...]).astype(y_ref.dtype)


def _rw_recur(r, k, v, wl, al, g, par, s0, *, n_seq, tg, t_valid):
    rows, d = r.shape
    c = min(64, tg)
    n_chunks = tg // c
    n_pairs = d // LANES
    tile = pl.BlockSpec((c, LANES), lambda b, p, ci: (b * n_chunks + ci, p))
    state = pl.BlockSpec((None, None, LANES, LANES), lambda b, p, ci: (b, p, 0, 0))
    return pl.pallas_call(
        functools.partial(_rw_recur_kernel, chunk=c, t_valid=t_valid),
        out_shape=[jax.ShapeDtypeStruct((rows, d), BF16),
                   jax.ShapeDtypeStruct((n_seq, n_pairs, LANES, LANES), F32)],
        grid=(n_seq, n_pairs, n_chunks),
        in_specs=[tile] * 6 + [pl.BlockSpec((8, LANES), lambda b, p, ci: (0, p)), state],
        out_specs=[tile, state],
        scratch_shapes=[pltpu.VMEM((LANES, LANES), F32)],
        compiler_params=_params("parallel", "parallel", "arbitrary"),
        name="rw_recur",
    )(r, k, v, wl, al, g, par, s0)


def _rw_state_to_pairs(wkv):
    b, h, n, _ = wkv.shape
    st = jnp.swapaxes(wkv, -1, -2).reshape(b, h // 2, 2, n, n)
    z = jnp.zeros_like(st[:, :, 0])
    top = jnp.concatenate([st[:, :, 0], z], axis=-1)
    bot = jnp.concatenate([z, st[:, :, 1]], axis=-1)
    return jnp.concatenate([top, bot], axis=-2)


def _rw_pairs_to_state(s):
    b, p, _, _ = s.shape
    n = RW_HEAD_DIM
    h0 = s[:, :, :n, :n]
    h1 = s[:, :, n:, n:]
    st = jnp.stack([h0, h1], axis=2).reshape(b, 2 * p, n, n)
    return jnp.swapaxes(st, -1, -2)


def _rwkv7(h, shift_prev, wkv_prev, w, *, n_seq, tg, t_valid):
    rows, d = h.shape
    h3 = h.reshape(n_seq, tg, d)
    h_prev = jnp.concatenate([shift_prev[:, None, :], h3[:, :-1]], axis=1).reshape(rows, d)
    xr, xw, xk, xv, xa, xg = _rw_mix(h, h_prev, w["rw_mu"])
    r = _matmul(xr, w["rw_w_r"])
    k = _matmul(xk, w["rw_w_k"])
    v = _matmul(xv, w["rw_w_v"])
    wl = _matmul(_matmul(xw, w["rw_w1"], out_dtype=BF16, act="tanh"), w["rw_w2"])
    al = _matmul(_matmul(xa, w["rw_a1"], out_dtype=BF16), w["rw_a2"])
    g = _matmul(_matmul(xg, w["rw_g1"], out_dtype=BF16, act="sigmoid"), w["rw_g2"])
    par = jnp.stack([w["rw_w0"], w["rw_a0"], w["rw_k_k"], w["rw_k_a"], w["rw_r_k"].reshape(-1),
                     w["rw_ln_w"], w["rw_ln_b"], jnp.zeros((d,), F32)])
    y, s_fin = _rw_recur(r, k, v, wl, al, g, par, _rw_state_to_pairs(wkv_prev),
                         n_seq=n_seq, tg=tg, t_valid=t_valid)
    return y, h3[:, t_valid - 1], _rw_pairs_to_state(s_fin)


def _sb_tile(z, mask, later_carry, u_mat):
    lk0 = -_softplus(z)
    lk = lk0 if mask is None else jnp.where(mask, lk0, 0.0)
    later = _dot_exact_rhs(lk, u_mat)
    w = jnp.exp(z + lk0 + later + later_carry)
    if mask is not None:
        w = jnp.where(mask, w, 0.0)
    return w, later_carry + later[:, 0:1] + lk[:, 0:1]


def _suffix_matrix(n):
    j = lax.broadcasted_iota(jnp.int32, (n, n), 0)
    k = lax.broadcasted_iota(jnp.int32, (n, n), 1)
    return jnp.where(j > k, 1.0, 0.0)


def _sb_prompt_kernel(q_ref, k_ref, v_ref, o_ref, *, bq):
    qi = pl.program_id(2)
    q = q_ref[...].astype(BF16)
    scale = SB_HEAD_DIM ** -0.5
    u_mat = _suffix_matrix(bq)
    row = lax.broadcasted_iota(jnp.int32, (bq, bq), 0)
    col = lax.broadcasted_iota(jnp.int32, (bq, bq), 1)

    def body(j, state):
        acc, carry = state
        kb = qi - j
        start = pl.multiple_of(kb * bq, bq)
        k_blk = k_ref[pl.ds(start, bq), :].astype(BF16)
        v_blk = v_ref[pl.ds(start, bq), :].astype(BF16)
        z = lax.dot_general(q, k_blk, _NT, preferred_element_type=F32) * scale
        mask = (col + kb * bq) < (row + qi * bq)
        w, carry = _sb_tile(z, mask, carry, u_mat)
        acc = acc + jnp.dot(w.astype(BF16), v_blk, preferred_element_type=F32)
        return acc, carry

    acc, _ = lax.fori_loop(0, qi + 1, body, (jnp.zeros((bq, SB_HEAD_DIM), F32), jnp.zeros((bq, 1), F32)))
    o_ref[...] = acc.astype(o_ref.dtype)


def _sb_prompt(qkv, *, n_seq, tg):
    rows, d3 = qkv.shape
    d = d3 // 3
    hd = SB_HEAD_DIM
    n_heads = d // hd
    bq = min(256, tg)
    nq = tg // bq
    return pl.pallas_call(
        functools.partial(_sb_prompt_kernel, bq=bq),
        out_shape=jax.ShapeDtypeStruct((rows, d), BF16),
        grid=(n_seq, n_heads, nq),
        in_specs=[
            pl.BlockSpec((bq, hd), lambda b, h, qi: (b * nq + qi, h)),
            pl.BlockSpec((tg, hd), lambda b, h, qi: (b, n_heads + h)),
            pl.BlockSpec((tg, hd), lambda b, h, qi: (b, 2 * n_heads + h)),
        ],
        out_specs=pl.BlockSpec((bq, hd), lambda b, h, qi: (b * nq + qi, h)),
        compiler_params=_params("parallel", "parallel", "arbitrary"),
        name="sb_prompt",
    )(qkv, qkv, qkv)


def _sb_sample_kernel(pt_ref, q_ref, kn_ref, vn_ref, kc_ref, vc_ref, o_ref, acc_scr, carry_scr,
                      *, n_heads, t_valid, page):
    j = pl.program_id(1)
    hd = SB_HEAD_DIM
    slot = SAMPLE_SLOT
    scale = hd ** -0.5

    def head_slice(h):
        return pl.ds(pl.multiple_of(h * hd, hd), hd)

    @pl.when(j == 0)
    def _():
        row = lax.broadcasted_iota(jnp.int32, (slot, slot), 0)
        col = lax.broadcasted_iota(jnp.int32, (slot, slot), 1)
        mask = col < jnp.minimum(row, t_valid)
        u_mat = _suffix_matrix(slot)

        def body(h, _):
            sl = head_slice(h)
            z = _dot1(q_ref[:, sl], kn_ref[:, sl], _NT) * scale
            w, carry = _sb_tile(z, mask, jnp.zeros((slot, 1), F32), u_mat)
            acc_scr[h] = _dot1(w, vn_ref[:, sl])
            carry_scr[h] = jnp.broadcast_to(carry, (slot, LANES))
            return 0

        lax.fori_loop(0, n_heads, body, 0)

    @pl.when(j > 0)
    def _():
        u_mat = _suffix_matrix(page)

        def body(h, _):
            sl = head_slice(h)
            z = _dot1(q_ref[:, sl], kc_ref[:, sl], _NT) * scale
            w, carry = _sb_tile(z, None, carry_scr[h][:, 0:1], u_mat)
            acc_scr[h] = acc_scr[h] + _dot1(w, vc_ref[:, sl])
            carry_scr[h] = jnp.broadcast_to(carry, (slot, LANES))
            return 0

        lax.fori_loop(0, n_heads, body, 0)

    @pl.when(j == pl.num_programs(1) - 1)
    def _():
        def body(h, _):
            o_ref[:, head_slice(h)] = acc_scr[h].astype(o_ref.dtype)
            return 0

        lax.fori_loop(0, n_heads, body, 0)


def _paged_attention_call(kernel_fn, name, page_table, qkv, cache_k, cache_v, scratch, n_heads, t_valid):
    n_seq, n_pages = page_table.shape
    rows, d3 = qkv.shape
    d = d3 // 3
    n_phys, page = cache_k.shape[:2]
    ck = cache_k.reshape(n_phys, page, d)
    cv = cache_v.reshape(n_phys, page, d)
    slot = SAMPLE_SLOT

    def page_map(s, j, pt):
        return (pt[s, n_pages - jnp.maximum(j, 1)], 0, 0)

    grid_spec = pltpu.PrefetchScalarGridSpec(
        num_scalar_prefetch=1,
        grid=(n_seq, n_pages + 1),
        in_specs=[
            pl.BlockSpec((slot, d), lambda s, j, pt: (s, 0)),
            pl.BlockSpec((slot, d), lambda s, j, pt: (s, 1)),
            pl.BlockSpec((slot, d), lambda s, j, pt: (s, 2)),
            pl.BlockSpec((None, page, d), page_map),
            pl.BlockSpec((None, page, d), page_map),
        ],
        out_specs=pl.BlockSpec((slot, d), lambda s, j, pt: (s, 0)),
        scratch_shapes=scratch,
    )
    return pl.pallas_call(
        functools.partial(kernel_fn, n_heads=n_heads, t_valid=t_valid, page=page),
        out_shape=jax.ShapeDtypeStruct((rows, d), BF16),
        grid_spec=grid_spec,
        compiler_params=_params("parallel", "arbitrary"),
        name=name,
    )(page_table, qkv, qkv, qkv, ck, cv)


def _sb_sample(qkv, cache_k, cache_v, page_table, *, t_valid):
    d = qkv.shape[1] // 3
    n_heads = d // SB_HEAD_DIM
    scratch = [pltpu.VMEM((n_heads, SAMPLE_SLOT, SB_HEAD_DIM), F32),
               pltpu.VMEM((n_heads, SAMPLE_SLOT, LANES), F32)]
    return _paged_attention_call(_sb_sample_kernel, "sb_sample", page_table, qkv, cache_k, cache_v,
                                 scratch, n_heads, t_valid)


def _rope_kernel(x_ref, cs_ref, sn_ref, o_ref):
    cs = cs_ref[...]
    sn = sn_ref[...]
    for c in range(x_ref.shape[1] // LANES):
        x = x_ref[:, c * LANES:(c + 1) * LANES]
        o_ref[:, c * LANES:(c + 1) * LANES] = x * cs + pltpu.roll(x, LANES // 2, 1) * sn


def _rope_tables(pos):
    half = DA_HEAD_DIM // 2
    inv_freq = ROPE_THETA ** (-jnp.arange(half, dtype=F32) / half)
    ang = pos.astype(F32)[:, None] * inv_freq[None, :]
    cos, sin = jnp.cos(ang), jnp.sin(ang)
    return jnp.concatenate([cos, cos], axis=1), jnp.concatenate([-sin, sin], axis=1)


def _rope_qk(qkv, pos_rows):
    rows, d3 = qkv.shape
    d2 = d3 // 3 * 2
    cs, sn = _rope_tables(pos_rows)
    tt = cs.shape[0]
    tm = min(256, tt)
    tn = min(1024, d2)
    per = tt // tm
    return pl.pallas_call(
        _rope_kernel,
        out_shape=jax.ShapeDtypeStruct((rows, d2), F32),
        grid=(rows // tm, d2 // tn),
        in_specs=[
            pl.BlockSpec((tm, tn), lambda i, j: (i, j)),
            pl.BlockSpec((tm, LANES), lambda i, j: (i % per, 0)),
            pl.BlockSpec((tm, LANES), lambda i, j: (i % per, 0)),
        ],
        out_specs=pl.BlockSpec((tm, tn), lambda i, j: (i, j)),
        compiler_params=_params("parallel", "parallel"),
        name="rope",
    )(qkv, cs, sn)


def _softmax_tile(s, mask, m_prev, l_prev):
    if mask is not None:
        s = jnp.where(mask, s, -jnp.inf)
    m_new = jnp.maximum(m_prev, jnp.max(s, axis=-1, keepdims=True))
    p = jnp.exp(s - m_new)
    alpha = jnp.exp(m_prev - m_new)
    return p, m_new, alpha, alpha * l_prev + jnp.sum(p, axis=-1, keepdims=True)


def _da_finish(acc1, l1, acc2, l2, lam, subln, lam_init):
    o = acc1 / l1 - lam * (acc2 / l2)
    o = o * lax.rsqrt(jnp.mean(o * o, axis=-1, keepdims=True) + DA_SUBLN_EPS) * subln
    return o * (1.0 - lam_init)


def _da_prompt_kernel(lam_ref, q_ref, k_ref, v_ref, sub_ref, o_ref, *, bq, lam_init):
    qi = pl.program_id(2)
    hd = DA_HEAD_DIM
    scale = hd ** -0.5
    q1 = q_ref[:, :hd].astype(BF16)
    q2 = q_ref[:, hd:].astype(BF16)
    row = lax.broadcasted_iota(jnp.int32, (bq, bq), 0)
    col = lax.broadcasted_iota(jnp.int32, (bq, bq), 1)

    def body(kb, state):
        m1, l1, acc1, m2, l2, acc2 = state
        start = pl.multiple_of(kb * bq, bq)
        k_blk = k_ref[pl.ds(start, bq), :].astype(BF16)
        v_blk = v_ref[pl.ds(start, bq), :].astype(BF16)
        mask = (col + kb * bq) <= (row + qi * bq)
        s1 = lax.dot_general(q1, k_blk[:, :hd], _NT, preferred_element_type=F32) * scale
        s2 = lax.dot_general(q2, k_blk[:, hd:], _NT, preferred_element_type=F32) * scale
        p1, m1, a1, l1 = _softmax_tile(s1, mask, m1, l1)
        p2, m2, a2, l2 = _softmax_tile(s2, mask, m2, l2)
        acc1 = a1 * acc1 + jnp.dot(p1.astype(BF16), v_blk, preferred_element_type=F32)
        acc2 = a2 * acc2 + jnp.dot(p2.astype(BF16), v_blk, preferred_element_type=F32)
        return m1, l1, acc1, m2, l2, acc2

    neg = jnp.full((bq, 1), -jnp.inf, F32)
    zero = jnp.zeros((bq, 1), F32)
    zacc = jnp.zeros((bq, 2 * hd), F32)
    _, l1, acc1, _, l2, acc2 = lax.fori_loop(0, qi + 1, body, (neg, zero, zacc, neg, zero, zacc))
    o_ref[...] = _da_finish(acc1, l1, acc2, l2, lam_ref[0], sub_ref[...], lam_init).astype(o_ref.dtype)


def _da_prompt(qk, qkv, lam, subln, *, n_seq, tg, lam_init):
    rows, d2 = qk.shape
    d = d2 // 2
    hw = 2 * DA_HEAD_DIM
    n_heads = d // hw
    bq = min(256, tg)
    nq = tg // bq
    return pl.pallas_call(
        functools.partial(_da_prompt_kernel, bq=bq, lam_init=lam_init),
        out_shape=jax.ShapeDtypeStruct((rows, d), BF16),
        grid=(n_seq, n_heads, nq),
        in_specs=[
            pl.BlockSpec(memory_space=pltpu.SMEM),
            pl.BlockSpec((bq, hw), lambda b, h, qi: (b * nq + qi, h)),
            pl.BlockSpec((tg, hw), lambda b, h, qi: (b, n_heads + h)),
            pl.BlockSpec((tg, hw), lambda b, h, qi: (b, 2 * n_heads + h)),
            pl.BlockSpec((1, hw), lambda b, h, qi: (0, 0)),
        ],
        out_specs=pl.BlockSpec((bq, hw), lambda b, h, qi: (b * nq + qi, h)),
        compiler_params=_params("parallel", "parallel", "arbitrary"),
        name="da_prompt",
    )(lam, qk, qk, qkv, subln.reshape(1, hw))


def _da_sample_kernel(pt_ref, q_ref, kn_ref, vn_ref, kc_ref, vc_ref, lam_ref, sub_ref, o_ref,
                      m_scr, l_scr, acc_scr, *, n_heads, t_valid, page, lam_init):
    j = pl.program_id(1)
    hd = DA_HEAD_DIM
    hw = 2 * hd
    slot = SAMPLE_SLOT
    scale = hd ** -0.5

    def part(h, c):
        return pl.ds(pl.multiple_of(h * hw + c * hd, hd), hd)

    def both(h):
        return pl.ds(pl.multiple_of(h * hw, hw), hw)

    def step(k_ref, v_ref, mask, first):
        def body(h, _):
            v_blk = v_ref[:, both(h)]
            for c in range(2):
                idx = 2 * h + c
                s = _dot1(q_ref[:, part(h, c)], k_ref[:, part(h, c)], _NT) * scale
                if first:
                    m_prev = jnp.full((slot, 1), -jnp.inf, F32)
                    l_prev = jnp.zeros((slot, 1), F32)
                else:
                    m_prev = m_scr[idx][:, 0:1]
                    l_prev = l_scr[idx][:, 0:1]
                p, m_new, alpha, l_new = _softmax_tile(s, mask, m_prev, l_prev)
                pv = _dot1(p, v_blk)
                acc_scr[idx] = pv if first else alpha * acc_scr[idx] + pv
                m_scr[idx] = jnp.broadcast_to(m_new, (slot, LANES))
                l_scr[idx] = jnp.broadcast_to(l_new, (slot, LANES))
            return 0

        lax.fori_loop(0, n_heads, body, 0)

    @pl.when(j == 0)
    def _():
        row = lax.broadcasted_iota(jnp.int32, (slot, slot), 0)
        col = lax.broadcasted_iota(jnp.int32, (slot, slot), 1)
        step(kn_ref, vn_ref, col <= jnp.minimum(row, t_valid - 1), True)

    @pl.when(j > 0)
    def _():
        step(kc_ref, vc_ref, None, False)

    @pl.when(j == pl.num_programs(1) - 1)
    def _():
        def body(h, _):
            o = _da_finish(acc_scr[2 * h], l_scr[2 * h][:, 0:1], acc_scr[2 * h + 1], l_scr[2 * h + 1][:, 0:1],
                           lam_ref[0], sub_ref[...], lam_init)
            o_ref[:, both(h)] = o.astype(o_ref.dtype)
            return 0

        lax.fori_loop(0, n_heads, body, 0)


def _da_sample(qk, qkv, cache_k, cache_v, page_table, lam, subln, *, t_valid, lam_init):
    rows, d2 = qk.shape
    d = d2 // 2
    hw = 2 * DA_HEAD_DIM
    n_heads = d // hw
    n_seq, n_pages = page_table.shape
    n_phys, page = cache_k.shape[:2]
    ck = cache_k.reshape(n_phys, page, d)
    cv = cache_v.reshape(n_phys, page, d)
    slot = SAMPLE_SLOT

    def page_map(s, j, pt):
        return (pt[s, n_pages - jnp.maximum(j, 1)], 0, 0)

    grid_spec = pltpu.PrefetchScalarGridSpec(
        num_scalar_prefetch=1,
        grid=(n_seq, n_pages + 1),
        in_specs=[
            pl.BlockSpec((slot, d), lambda s, j, pt: (s, 0)),
            pl.BlockSpec((slot, d), lambda s, j, pt: (s, 1)),
            pl.BlockSpec((slot, d), lambda s, j, pt: (s, 2)),
            pl.BlockSpec((None, page, d), page_map),
            pl.BlockSpec((None, page, d), page_map),
            pl.BlockSpec(memory_space=pltpu.SMEM),
            pl.BlockSpec((1, hw), lambda s, j, pt: (0, 0)),
        ],
        out_specs=pl.BlockSpec((slot, d), lambda s, j, pt: (s, 0)),
        scratch_shapes=[pltpu.VMEM((2 * n_heads, slot, LANES), F32),
                        pltpu.VMEM((2 * n_heads, slot, LANES), F32),
                        pltpu.VMEM((2 * n_heads, slot, hw), F32)],
    )
    return pl.pallas_call(
        functools.partial(_da_sample_kernel, n_heads=n_heads, t_valid=t_valid, page=page, lam_init=lam_init),
        out_shape=jax.ShapeDtypeStruct((rows, d), BF16),
        grid_spec=grid_spec,
        compiler_params=_params("parallel", "arbitrary"),
        name="da_sample",
    )(page_table, qk, qk, qkv, ck, cv, lam, subln.reshape(1, hw))


def _gla_kernel(q_ref, k_ref, v_ref, gate_ref, lo_ref, ba_ref, nw_ref, s0_ref, o_ref, sout_ref, s_scr,
                *, tb, c, tg, t_valid, dk):
    ci = pl.program_id(2)

    @pl.when(ci == 0)
    def _():
        s_scr[...] = s0_ref[...]

    ti = lax.broadcasted_iota(jnp.int32, (c, c), 0)
    si = lax.broadcasted_iota(jnp.int32, (c, c), 1)
    tri = jnp.where(ti >= si, 1.0, 0.0)
    row_id = lax.broadcasted_iota(jnp.int32, (c, 1), 0)
    ones_cl = jnp.ones((c, LANES), F32)
    ba = ba_ref[...]
    nw = nw_ref[...]
    for sub in range(tb // c):
        rows = slice(sub * c, (sub + 1) * c)
        q = q_ref[rows, :] * (dk ** -0.5)
        k = k_ref[rows, :]
        v = v_ref[rows, :]
        la = -_softplus(-(lo_ref[rows, :] + ba)) / GLA_TAU
        if t_valid < tg:
            valid = (ci * tb + sub * c + row_id) < t_valid
            la = jnp.where(valid, la, 0.0)
            k = jnp.where(valid, k, 0.0)
        cum = _dot_exact_lhs(tri, la)
        cum_end = cum[c - 1:c, :]
        s = s_scr[...]
        o = _dot1(q * jnp.exp(cum), s)
        att = jnp.zeros((c, c), F32)
        for j in range(c):
            e = jnp.exp(jnp.where(row_id >= j, cum - cum[j:j + 1, :], -jnp.inf))
            p = jnp.sum(q * k[j:j + 1, :] * e, axis=-1, keepdims=True)
            att = jnp.where(si == j, p, att)
        o = o + _dot1(att, v)
        dec = jnp.exp(_dot_exact_rhs(la, ones_cl, _TN)[:, 0:1])
        s_scr[...] = s * dec + _dot1(k * jnp.exp(cum_end - cum), v, _TN)
        on = o * lax.rsqrt(jnp.mean(o * o, axis=-1, keepdims=True) + GLA_NORM_EPS) * nw
        g = gate_ref[rows, :]
        o_ref[rows, :] = (on * (g * _sigmoid(g))).astype(o_ref.dtype)

    @pl.when(ci == pl.num_programs(2) - 1)
    def _():
        sout_ref[...] = s_scr[...]


def _gla(proj, lo, ba, norm_w, s0, *, n_seq, tg, t_valid):
    rows = proj.shape[0]
    key = lo.shape[1]
    d = (proj.shape[1] - 2 * key) // 2
    nh = GLA_HEADS
    dk, dv = key // nh, d // nh
    tb = min(64, tg)
    c = min(16, tb)
    nc = tg // tb
    kv = key // dv
    return pl.pallas_call(
        functools.partial(_gla_kernel, tb=tb, c=c, tg=tg, t_valid=t_valid, dk=dk),
        out_shape=[jax.ShapeDtypeStruct((rows, d), BF16), jax.ShapeDtypeStruct((n_seq, nh, dk, dv), F32)],
        grid=(n_seq, nh, nc),
        in_specs=[
            pl.BlockSpec((tb, dk), lambda b, h, ci: (b * nc + ci, h)),
            pl.BlockSpec((tb, dk), lambda b, h, ci: (b * nc + ci, nh + h)),
            pl.BlockSpec((tb, dv), lambda b, h, ci: (b * nc + ci, 2 * kv + h)),
            pl.BlockSpec((tb, dv), lambda b, h, ci: (b * nc + ci, 2 * kv + nh + h)),
            pl.BlockSpec((tb, dk), lambda b, h, ci: (b * nc + ci, h)),
            pl.BlockSpec((1, dk), lambda b, h, ci: (0, h)),
            pl.BlockSpec((1, dv), lambda b, h, ci: (0, 0)),
            pl.BlockSpec((None, None, dk, dv), lambda b, h, ci: (b, h, 0, 0)),
        ],
        out_specs=[pl.BlockSpec((tb, dv), lambda b, h, ci: (b * nc + ci, h)),
                   pl.BlockSpec((None, None, dk, dv), lambda b, h, ci: (b, h, 0, 0))],
        scratch_shapes=[pltpu.VMEM((dk, dv), F32)],
        compiler_params=_params("parallel", "parallel", "arbitrary"),
        name="gla",
    )(proj, proj, proj, proj, lo, ba.reshape(1, key), norm_w.reshape(1, dv), s0)


MOE_BLOCK_ROWS = 256


def _moe_route(logits, n_groups, n_experts):
    g_logits = logits[:, :n_groups]
    grp = jnp.argmax(g_logits, axis=-1)
    g_w = jnp.take_along_axis(jax.nn.softmax(g_logits, axis=-1), grp[:, None], axis=-1)
    e_logits = logits[:, n_groups:n_groups + n_groups * n_experts].reshape(-1, n_groups, n_experts)
    e_logits = jnp.take_along_axis(e_logits, grp[:, None, None], axis=1)[:, 0]
    top_val, top_idx = lax.top_k(e_logits, MOE_TOPK)
    gate = g_w * jax.nn.softmax(top_val, axis=-1)
    return (grp[:, None] * n_experts + top_idx).astype(jnp.int32), gate


def _moe_plan(expert_id, n_exp):
    tm = MOE_BLOCK_ROWS
    n_tok, top_k = expert_id.shape
    n_assign = n_tok * top_k
    e_flat = expert_id.reshape(-1)
    order = jnp.argsort(e_flat)
    e_sorted = e_flat[order]
    counts = jnp.bincount(e_flat, length=n_exp)
    padded = (counts + tm - 1) // tm * tm
    pad_end = jnp.cumsum(padded)
    rank = jnp.arange(n_assign) - (jnp.cumsum(counts) - counts)[e_sorted]
    dest = ((pad_end - padded)[e_sorted] + rank).astype(jnp.int32)
    n_blocks = -(-n_assign // tm) + n_exp
    src_tok = jnp.zeros((n_blocks * tm,), jnp.int32).at[dest].set((order // top_k).astype(jnp.int32))
    pos = jnp.zeros((n_assign,), jnp.int32).at[order].set(dest).reshape(n_tok, top_k)
    block_expert = jnp.minimum(jnp.searchsorted(pad_end, jnp.arange(n_blocks) * tm, side="right"), n_exp - 1)
    n_used = (pad_end[-1] // tm).astype(jnp.int32).reshape(1)
    return src_tok, pos, block_expert.astype(jnp.int32), n_used


def _moe_up_kernel(be_ref, nu_ref, x_ref, wg_ref, wu_ref, h_ref):
    b = pl.program_id(0)

    @pl.when(b < nu_ref[0])
    def _():
        x = x_ref[...]
        g = jnp.dot(x, wg_ref[...].astype(BF16), preferred_element_type=F32)
        u = jnp.dot(x, wu_ref[...].astype(BF16), preferred_element_type=F32)
        h_ref[...] = (g * _sigmoid(g) * u).astype(h_ref.dtype)

    @pl.when(b >= nu_ref[0])
    def _():
        h_ref[...] = jnp.zeros_like(h_ref)


def _moe_down_kernel(be_ref, nu_ref, h_ref, wd_ref, o_ref):
    b = pl.program_id(0)

    @pl.when(b < nu_ref[0])
    def _():
        o_ref[...] = jnp.dot(h_ref[...], wd_ref[...].astype(BF16), preferred_element_type=F32)

    @pl.when(b >= nu_ref[0])
    def _():
        o_ref[...] = jnp.zeros_like(o_ref)


def _moe_experts(rows, block_expert, n_used, w_gate, w_up, w_down):
    tm = MOE_BLOCK_ROWS
    n_rows, d = rows.shape
    f = w_gate.shape[2]
    n_blocks = n_rows // tm
    up_spec = pltpu.PrefetchScalarGridSpec(
        num_scalar_prefetch=2,
        grid=(n_blocks,),
        in_specs=[
            pl.BlockSpec((tm, d), lambda b, be, nu: (b, 0)),
            pl.BlockSpec((None, d, f), lambda b, be, nu: (be[b], 0, 0)),
            pl.BlockSpec((None, d, f), lambda b, be, nu: (be[b], 0, 0)),
        ],
        out_specs=pl.BlockSpec((tm, f), lambda b, be, nu: (b, 0)),
    )
    h = pl.pallas_call(
        _moe_up_kernel,
        out_shape=jax.ShapeDtypeStruct((n_rows, f), BF16),
        grid_spec=up_spec,
        compiler_params=_params("arbitrary"),
        name="moe_up",
    )(block_expert, n_used, rows, w_gate, w_up)
    down_spec = pltpu.PrefetchScalarGridSpec(
        num_scalar_prefetch=2,
        grid=(n_blocks,),
        in_specs=[
            pl.BlockSpec((tm, f), lambda b, be, nu: (b, 0)),
            pl.BlockSpec((None, f, d), lambda b, be, nu: (be[b], 0, 0)),
        ],
        out_specs=pl.BlockSpec((tm, d), lambda b, be, nu: (b, 0)),
    )
    return pl.pallas_call(
        _moe_down_kernel,
        out_shape=jax.ShapeDtypeStruct((n_rows, d), F32),
        grid_spec=down_spec,
        compiler_params=_params("arbitrary"),
        name="moe_down",
    )(block_expert, n_used, h, w_down)


def _moe_combine_kernel(x_ref, o0_ref, o1_ref, g_ref, m_ref, y_ref):
    g = g_ref[...]
    y = o0_ref[...] * g[:, 0:1] + o1_ref[...] * g[:, 1:2]
    y_ref[...] = x_ref[...] + m_ref[...] * y


def _moe_combine(x, o0, o1, gates, gmod, *, tg):
    rows, d = x.shape
    tm = min(256, tg) if gmod.shape[1] == 1 else rows
    tile = pl.BlockSpec((tm, d), lambda i: (i, 0))
    return pl.pallas_call(
        _moe_combine_kernel,
        out_shape=jax.ShapeDtypeStruct((rows, d), F32),
        grid=(rows // tm,),
        in_specs=[tile, tile, tile, pl.BlockSpec((tm, MOE_TOPK), lambda i: (i, 0)), _mod_spec(gmod, tm, tg, d)],
        out_specs=tile,
        compiler_params=_params("parallel"),
        name="moe_combine",
    )(x, o0, o1, gates, gmod)


def _moe_layer(groups, xs, hs, logits, gmods, w_group, w_gate, w_up, w_down):
    n_groups, n_experts = w_group.shape[1], w_gate.shape[0] // w_group.shape[1]
    d = xs[0].shape[1]
    real_h, real_l = [], []
    for grp, h, lg in zip(groups, hs, logits):
        n_seq, tg, tv = grp
        real_h.append(h.reshape(n_seq, tg, d)[:, :tv].reshape(n_seq * tv, d))
        real_l.append(lg.reshape(n_seq, tg, -1)[:, :tv].reshape(n_seq * tv, -1))
    h_all = jnp.concatenate(real_h, axis=0)
    expert_id, gate = _moe_route(jnp.concatenate(real_l, axis=0), n_groups, n_experts)
    src_tok, pos, block_expert, n_used = _moe_plan(expert_id, n_groups * n_experts)
    out = _moe_experts(h_all[src_tok], block_expert, n_used, w_gate, w_up, w_down)
    o0, o1 = out[pos[:, 0]], out[pos[:, 1]]
    new_xs = []
    start = 0
    for grp, x, gmod in zip(groups, xs, gmods):
        n_seq, tg, tv = grp
        n = n_seq * tv
        parts = []
        for a in (o0[start:start + n], o1[start:start + n], gate[start:start + n]):
            a = a.reshape(n_seq, tv, -1)
            if tv < tg:
                a = jnp.pad(a, ((0, 0), (0, tg - tv), (0, 0)))
            parts.append(a.reshape(n_seq * tg, -1))
        new_xs.append(_moe_combine(x, parts[0], parts[1], parts[2], gmod, tg=tg))
        start += n
    return new_xs


def kernel(x_prompt, x_sample, c_prompt, c_sample, state_rwkv_shift, state_rwkv_wkv, cache_sb_k, cache_sb_v, cache_da_k, cache_da_v, state_gla, page_table, ada_w, ada_b, norm_mix, norm_ffn, norm_final, rw_mu, rw_w_r, rw_w_k, rw_w_v, rw_w_o, rw_w0, rw_w1, rw_w2, rw_a0, rw_a1, rw_a2, rw_g1, rw_g2, rw_k_k, rw_k_a, rw_r_k, rw_ln_w, rw_ln_b, sb_w_in, sb_w_out, da_w_in, da_w_out, da_lambda, da_subln, gla_w_in, gla_wa1, gla_wa2, gla_ba, gla_norm, gla_w_out, moe_w_group, moe_w_expert, moe_w_gate, moe_w_up, moe_w_down):
    n_p, t_p, d = x_prompt.shape
    n_s, t_s, _ = x_sample.shape
    depth = ada_w.shape[0]
    slot = SAMPLE_SLOT
    n_pages, page = page_table.shape[1], cache_sb_k.shape[1]
    past_len = n_pages * page
    groups = [(n_p, t_p, t_p), (n_s, slot, t_s)]

    n_c = n_p + n_s
    c_all = jnp.concatenate([c_prompt, c_sample], axis=0)
    c_all = jnp.pad(c_all, ((0, -n_c % SUBLANES), (0, 0)))
    mod = _adaln(c_all, ada_w, ada_b)

    xs = [x_prompt.reshape(n_p * t_p, d),
          jnp.pad(x_sample, ((0, 0), (0, slot - t_s), (0, 0))).reshape(n_s * slot, d)]
    rw = dict(rw_mu=rw_mu, rw_w0=rw_w0, rw_a0=rw_a0, rw_k_k=rw_k_k, rw_k_a=rw_k_a, rw_r_k=rw_r_k,
              rw_ln_w=rw_ln_w, rw_ln_b=rw_ln_b)
    for name, val in (("rw_w_r", rw_w_r), ("rw_w_k", rw_w_k), ("rw_w_v", rw_w_v), ("rw_w1", rw_w1),
                      ("rw_w2", rw_w2), ("rw_a1", rw_a1), ("rw_a2", rw_a2), ("rw_g1", rw_g1), ("rw_g2", rw_g2)):
        rw[name] = val.astype(BF16)
    pos = [jnp.arange(t_p, dtype=jnp.int32), jnp.tile(past_len + jnp.arange(slot, dtype=jnp.int32), n_s)]
    rw_state = [(jnp.zeros((n_p, d), F32), jnp.zeros((n_p, d // RW_HEAD_DIM, RW_HEAD_DIM, RW_HEAD_DIM), F32)),
                (state_rwkv_shift, state_rwkv_wkv)]
    gla_state = [jnp.zeros((n_p,) + state_gla.shape[1:], F32), state_gla]
    outs = [dict(), dict()]

    for i in range(depth):
        mods = []
        for gi, (n_seq, tg, tv) in enumerate(groups):
            rows = mod[i, :n_p] if gi == 0 else mod[i, n_p:n_c]
            six = [rows[:, j * d:(j + 1) * d] for j in range(6)]
            if gi == 0:
                mods.append([m[:, None, :] for m in six])
            else:
                mods.append([jnp.repeat(m, tg, axis=0)[None] for m in six])
        kind = i % 4
        for gi, (n_seq, tg, tv) in enumerate(groups):
            m = mods[gi]
            x = xs[gi]
            o = outs[gi]

            def rows_of(a, width_shape):
                return a.reshape((n_seq, tg) + width_shape)[:, :tv]

            if kind == 0:
                (h,) = _norm_mod(x, norm_mix[i], m[0], m[1], tg=tg, out_dtypes=(F32,))
                y, o["rw_shift"], o["rw_wkv"] = _rwkv7(h, rw_state[gi][0], rw_state[gi][1], rw,
                                                       n_seq=n_seq, tg=tg, t_valid=tv)
                w_out = rw_w_o
            elif kind == 1:
                (h,) = _norm_mod(x, norm_mix[i], m[0], m[1], tg=tg, out_dtypes=(BF16,))
                qkv = _matmul(h, sb_w_in.astype(BF16))
                n_h = d // SB_HEAD_DIM
                o["sb_k"] = rows_of(qkv[:, d:2 * d], (n_h, SB_HEAD_DIM))
                o["sb_v"] = rows_of(qkv[:, 2 * d:], (n_h, SB_HEAD_DIM))
                if gi == 0:
                    y = _sb_prompt(qkv, n_seq=n_seq, tg=tg)
                else:
                    y = _sb_sample(qkv, cache_sb_k, cache_sb_v, page_table, t_valid=tv)
                w_out = sb_w_out
            elif kind == 2:
                lam_init = 0.8 - 0.6 * math.exp(-0.3 * i)
                lf = da_lambda.astype(F32)
                lam = (jnp.exp(jnp.sum(lf[0] * lf[1])) - jnp.exp(jnp.sum(lf[2] * lf[3])) + lam_init).reshape(1)
                (h,) = _norm_mod(x, norm_mix[i], m[0], m[1], tg=tg, out_dtypes=(BF16,))
                qkv = _matmul(h, da_w_in.astype(BF16))
                qk = _rope_qk(qkv, pos[gi])
                n_h = d // (2 * DA_HEAD_DIM)
                o["da_k"] = rows_of(qk[:, d:], (n_h, 2, DA_HEAD_DIM))
                o["da_v"] = rows_of(qkv[:, 2 * d:], (n_h, 2 * DA_HEAD_DIM))
                if gi == 0:
                    y = _da_prompt(qk, qkv, lam, da_subln, n_seq=n_seq, tg=tg, lam_init=lam_init)
                else:
                    y = _da_sample(qk, qkv, cache_da_k, cache_da_v, page_table, lam, da_subln,
                                   t_valid=tv, lam_init=lam_init)
                w_out = da_w_out
            else:
                (h,) = _norm_mod(x, norm_mix[i], m[0], m[1], tg=tg, out_dtypes=(BF16,))
                proj = _matmul(h, gla_w_in.astype(BF16))
                lo = _matmul(_matmul(h, gla_wa1.astype(BF16), out_dtype=BF16), gla_wa2.astype(BF16))
                y, o["gla"] = _gla(proj, lo, gla_ba, gla_norm, gla_state[gi], n_seq=n_seq, tg=tg, t_valid=tv)
                w_out = gla_w_out
            xs[gi] = _matmul(y, w_out.astype(BF16), resid=x, gate=m[2], tg=tg)

        n_g, n_e = moe_w_group.shape[2], moe_w_expert.shape[3]
        router_w = jnp.concatenate([moe_w_group[i]] + [moe_w_expert[i, g] for g in range(n_g)], axis=1)
        router_w = jnp.pad(router_w, ((0, 0), (0, -router_w.shape[1] % LANES)))
        hs, logits = [], []
        for gi, (n_seq, tg, tv) in enumerate(groups):
            m = mods[gi]
            h, lg = _norm_mod(xs[gi], norm_ffn[i], m[3], m[4], tg=tg, out_dtypes=(BF16,), router_w=router_w)
            hs.append(h)
            logits.append(lg)
        xs = _moe_layer(groups, xs, hs, logits, [mods[0][5], mods[1][5]], moe_w_group[i],
                        moe_w_gate[i], moe_w_up[i], moe_w_down[i])

    ys = []
    for gi, (n_seq, tg, tv) in enumerate(groups):
        (y,) = _norm_mod(xs[gi], norm_final, None, None, tg=tg, out_dtypes=(F32,))
        ys.append(y.reshape(n_seq, tg, d)[:, :tv])
    names = ("rw_shift", "rw_wkv", "sb_k", "sb_v", "da_k", "da_v", "gla")
    return (ys[0], ys[1]) + tuple(outs[0][n] for n in names) + tuple(outs[1][n] for n in names)
```

```python
import functools
import math

import jax
import jax.numpy as jnp
from jax import lax
from jax.experimental import pallas as pl
from jax.experimental.pallas import tpu as pltpu

F32 = jnp.float32
BF16 = jnp.bfloat16

LANES = 128
SUBLANES = 8
VMEM_LIMIT_BYTES = 56 * 1024 * 1024

NORM_EPS = 1e-6
ROPE_THETA = 10000.0
RW_HEAD_DIM = 64
RW_GN_EPS = 64e-5
SB_HEAD_DIM = 128
DA_HEAD_DIM = 128
DA_SUBLN_EPS = 1e-5
GLA_HEADS = 4
GLA_TAU = 16.0
GLA_NORM_EPS = 1e-5
MOE_TOPK = 2
SAMPLE_SLOT = 8


def _params(*semantics):
    return pltpu.CompilerParams(dimension_semantics=semantics, vmem_limit_bytes=VMEM_LIMIT_BYTES)


def _split3(x):
    hi = x.astype(BF16)
    r1 = x - hi.astype(F32)
    mid = r1.astype(BF16)
    lo = (r1 - mid.astype(F32)).astype(BF16)
    return hi, mid, lo


_NN = (((1,), (0,)), ((), ()))
_NT = (((1,), (1,)), ((), ()))
_TN = (((0,), (0,)), ((), ()))


def _dot1(a, b, dims=_NN):
    return lax.dot_general(a.astype(BF16), b.astype(BF16), dims, preferred_element_type=F32)


def _dot_exact_rhs(a, b_exact, dims=_NN):
    hi, mid, lo = _split3(a)
    b = b_exact.astype(BF16)
    return (lax.dot_general(hi, b, dims, preferred_element_type=F32)
            + lax.dot_general(mid, b, dims, preferred_element_type=F32)
            + lax.dot_general(lo, b, dims, preferred_element_type=F32))


def _dot_exact_lhs(a_exact, b, dims=_NN):
    hi, mid, lo = _split3(b)
    a = a_exact.astype(BF16)
    return (lax.dot_general(a, hi, dims, preferred_element_type=F32)
            + lax.dot_general(a, mid, dims, preferred_element_type=F32)
            + lax.dot_general(a, lo, dims, preferred_element_type=F32))


def _dot3(a, b, dims=_NN):
    ah, am, _ = _split3(a)
    bh, bm, _ = _split3(b)
    d = functools.partial(lax.dot_general, dimension_numbers=dims, preferred_element_type=F32)
    return d(ah, bh) + (d(ah, bm) + d(am, bh))


def _softplus(x):
    return jnp.maximum(x, 0.0) + jnp.log1p(jnp.exp(-jnp.abs(x)))


def _sigmoid(x):
    return 1.0 / (1.0 + jnp.exp(-x))


def _adaln_kernel(c_ref, w_ref, b_ref, o_ref):
    c = c_ref[...]
    s = c * _sigmoid(c)
    o_ref[...] = _dot3(s, w_ref[...]) + b_ref[...]


def _adaln(c_all, ada_w, ada_b):
    n_layers, d, n = ada_w.shape
    rows = c_all.shape[0]
    tn = min(512, n)
    return pl.pallas_call(
        _adaln_kernel,
        out_shape=jax.ShapeDtypeStruct((n_layers, rows, n), F32),
        grid=(n_layers, n // tn),
        in_specs=[
            pl.BlockSpec((rows, d), lambda l, j: (0, 0)),
            pl.BlockSpec((None, d, tn), lambda l, j: (l, 0, j)),
            pl.BlockSpec((None, 1, tn), lambda l, j: (l, 0, j)),
        ],
        out_specs=pl.BlockSpec((None, rows, tn), lambda l, j: (l, 0, j)),
        compiler_params=_params("parallel", "parallel"),
        name="adaln",
    )(c_all, ada_w, ada_b.reshape(n_layers, 1, n))


def _norm_mod_kernel(*refs, eps, modulate, n_out, router):
    x_ref, w_ref = refs[:2]
    pos = 2
    x = x_ref[...]
    y = x * lax.rsqrt(jnp.mean(x * x, axis=-1, keepdims=True) + eps) * w_ref[...]
    if modulate:
        shift_ref, scale_ref = refs[pos:pos + 2]
        pos += 2
        y = y * (1.0 + scale_ref[...]) + shift_ref[...]
    if router:
        wr_ref = refs[pos]
        pos += 1
    outs = refs[pos:]
    for o in outs[:n_out]:
        o[...] = y.astype(o.dtype)
    if router:
        outs[n_out][...] = _dot3(y, wr_ref[...])


def _mod_spec(mod, tm, tg, width):
    g, gr, _ = mod.shape
    if gr == 1:
        per = tg // tm
        return pl.BlockSpec((None, 1, width), lambda i, *_: (i // per, 0, 0))
    assert g == 1 and gr == tm
    return pl.BlockSpec((None, gr, width), lambda i, *_: (0, 0, 0))


def _norm_mod(x, w, shift, scale, *, tg, out_dtypes, eps=NORM_EPS, router_w=None):
    rows, d = x.shape
    tm = min(256, tg) if shift is None or shift.shape[1] == 1 else rows
    in_specs = [pl.BlockSpec((tm, d), lambda i: (i, 0)), pl.BlockSpec((1, d), lambda i: (0, 0))]
    args = [x, w.reshape(1, d)]
    if shift is not None:
        in_specs += [_mod_spec(shift, tm, tg, d), _mod_spec(scale, tm, tg, d)]
        args += [shift, scale]
    out_shape = [jax.ShapeDtypeStruct((rows, d), dt) for dt in out_dtypes]
    out_specs = [pl.BlockSpec((tm, d), lambda i: (i, 0)) for _ in out_dtypes]
    if router_w is not None:
        nr = router_w.shape[1]
        in_specs.append(pl.BlockSpec((d, nr), lambda i: (0, 0)))
        args.append(router_w)
        out_shape.append(jax.ShapeDtypeStruct((rows, nr), F32))
        out_specs.append(pl.BlockSpec((tm, nr), lambda i: (i, 0)))
    return pl.pallas_call(
        functools.partial(_norm_mod_kernel, eps=eps, modulate=shift is not None,
                          n_out=len(out_dtypes), router=router_w is not None),
        out_shape=out_shape,
        grid=(rows // tm,),
        in_specs=in_specs,
        out_specs=out_specs,
        compiler_params=_params("parallel"),
        name="norm_mod",
    )(*args)


def _matmul_kernel(*refs, act, resid):
    a_ref, b_ref = refs[:2]
    o_ref = refs[-1]
    acc = jnp.dot(a_ref[...], b_ref[...], preferred_element_type=F32)
    if act == "tanh":
        acc = jnp.tanh(acc)
    elif act == "sigmoid":
        acc = _sigmoid(acc)
    if resid:
        x_ref, g_ref = refs[2:4]
        acc = x_ref[...] + g_ref[...] * acc
    o_ref[...] = acc.astype(o_ref.dtype)


def _matmul(a, b, *, out_dtype=F32, act=None, resid=None, gate=None, tg=None):
    m, k = a.shape
    n = b.shape[1]
    tm = min(1024, m)
    if gate is not None:
        tm = min(tm, tg) if gate.shape[1] == 1 else m
    tn = min(512, n)
    in_specs = [pl.BlockSpec((tm, k), lambda i, j: (i, 0)), pl.BlockSpec((k, tn), lambda i, j: (0, j))]
    args = [a, b]
    if resid is not None:
        g, gr, _ = gate.shape
        in_specs.append(pl.BlockSpec((tm, tn), lambda i, j: (i, j)))
        if gr == 1:
            per = tg // tm
            in_specs.append(pl.BlockSpec((None, 1, tn), lambda i, j: (i // per, 0, j)))
        else:
            in_specs.append(pl.BlockSpec((None, gr, tn), lambda i, j: (0, 0, j)))
        args += [resid, gate]
    return pl.pallas_call(
        functools.partial(_matmul_kernel, act=act, resid=resid is not None),
        out_shape=jax.ShapeDtypeStruct((m, n), out_dtype),
        grid=(m // tm, n // tn),
        in_specs=in_specs,
        out_specs=pl.BlockSpec((tm, tn), lambda i, j: (i, j)),
        compiler_params=_params("parallel", "parallel"),
        name="matmul",
    )(*args)


def _rw_mix_kernel(h_ref, hp_ref, mu_ref, *o_refs):
    h = h_ref[...]
    xx = hp_ref[...] - h
    for i, o in enumerate(o_refs):
        o[...] = (h + xx * mu_ref[i:i + 1, :]).astype(o.dtype)


def _rw_mix(h, h_prev, mu):
    rows, d = h.shape
    tm = min(256, rows)
    n = mu.shape[0]
    spec = pl.BlockSpec((tm, d), lambda i: (i, 0))
    return pl.pallas_call(
        _rw_mix_kernel,
        out_shape=[jax.ShapeDtypeStruct((rows, d), BF16)] * n,
        grid=(rows // tm,),
        in_specs=[spec, spec, pl.BlockSpec((n, d), lambda i: (0, 0))],
        out_specs=[spec] * n,
        compiler_params=_params("parallel"),
        name="rw_mix",
    )(h, h_prev, mu)


def _rw_recur_kernel(r_ref, k_ref, v_ref, wl_ref, al_ref, g_ref, par_ref, s0_ref,
                     y_ref, sout_ref, s_scr, *, chunk, n_sub, t_valid):
    c = chunk
    n2 = 2 * c
    bi = pl.program_id(2)
    hd = RW_HEAD_DIM

    @pl.when(bi == 0)
    def _():
        s_scr[...] = s0_ref[...]

    par = par_ref[...]
    w0, a0, k_k, k_a, r_k, ln_w, ln_b = (par[i:i + 1, :] for i in range(7))
    lane = lax.broadcasted_iota(jnp.int32, (1, LANES), 1)
    head0 = lane < hd
    lane_r = lax.broadcasted_iota(jnp.int32, (LANES, LANES), 0)
    lane_c = lax.broadcasted_iota(jnp.int32, (LANES, LANES), 1)
    same_head = (lane_r < hd) == (lane_c < hd)
    seg = jnp.where(same_head, 1.0, 0.0)
    ones_cl = jnp.ones((c, LANES), F32)
    ti = lax.broadcasted_iota(jnp.int32, (c, c), 0)
    si = lax.broadcasted_iota(jnp.int32, (c, c), 1)
    tri = jnp.where(ti >= si, 1.0, 0.0)
    rs = lax.broadcasted_iota(jnp.int32, (n2, n2), 0)
    cs = lax.broadcasted_iota(jnp.int32, (n2, n2), 1)
    same_blk = (rs < c) == (cs < c)
    rt = jnp.where(rs < c, rs, rs - c)
    ct = jnp.where(cs < c, cs, cs - c)
    low_s = jnp.logical_and(same_blk, rt > ct)
    low_i = jnp.logical_and(same_blk, rt >= ct)
    eye_n = jnp.where(rs == cs, 1.0, 0.0)
    row_id = lax.broadcasted_iota(jnp.int32, (c, 1), 0)

    def stack(x):
        return jnp.concatenate([x, x], axis=0)

    def stack_heads(x):
        return jnp.concatenate([jnp.where(head0, x, 0.0), jnp.where(head0, 0.0, x)], axis=0)

    def halves(x):
        return jnp.where(head0, x[:c], x[c:])

    subs = range(n_sub)
    rows = [slice(i * c, (i + 1) * c) for i in subs]
    each = lambda f, *xs: [f(*a) for a in zip(*xs)]
    r = [r_ref[rw, :] for rw in rows]
    k = [k_ref[rw, :] for rw in rows]
    v = [v_ref[rw, :] for rw in rows]
    lw = [-jnp.exp(-_softplus(-(w0 + wl_ref[rw, :])) - 0.5) for rw in rows]
    a = [_sigmoid(a0 + al_ref[rw, :]) for rw in rows]
    kk = each(lambda k_: k_ * k_k, k)
    nrm = each(lambda x: jnp.sqrt(_dot_exact_rhs(x * x, seg)), kk)
    kk = each(lambda x, n: x / jnp.maximum(n, 1e-12), kk, nrm)
    kmod = each(lambda k_, a_: k_ * (1.0 + (a_ - 1.0) * k_a), k, a)
    al = each(lambda x: -x, kk)
    be = each(lambda x, a_: x * a_, kk, a)
    kmod_s = kmod
    if t_valid < c * n_sub:
        valid = [(bi * (c * n_sub) + i * c + row_id) < t_valid for i in subs]
        mask = lambda x, m: jnp.where(m, x, 0.0)
        lw, al, be, kmod_s = each(mask, lw, valid), each(mask, al, valid), each(mask, be, valid), each(mask, kmod, valid)

    cum = each(lambda x: _dot_exact_lhs(tri, x), lw)
    p_col = each(lambda x: jnp.exp(_dot_exact_rhs(x, ones_cl, _TN)), lw)
    p_i = each(lambda x: jnp.exp(-x), cum)
    p_e = each(lambda x: jnp.exp(x[c - 1:c, :] - x), cum)
    a_t = each(lambda al_, cu, lw_: al_ * jnp.exp(cu - lw_), al, cum, lw)
    b_t = each(lambda x, p: x * p, be, p_i)
    k_t = each(lambda x, p: x * p, kmod_s, p_i)
    r_t = each(lambda x, cu: x * jnp.exp(cu), r, cum)
    b_bar = each(lambda x, p: x * p, be, p_e)
    k_bar = each(lambda x, p: x * p, kmod_s, p_e)

    x_a = each(stack_heads, a_t)
    x_r = each(stack_heads, r_t)
    y_b = each(stack, b_t)
    y_k = each(stack, k_t)
    v_st = each(stack, v)
    l_ab = each(lambda x, y: jnp.where(low_s, _dot1(x, y, _NT), 0.0), x_a, y_b)
    l_ak = each(lambda x, y: jnp.where(low_s, _dot1(x, y, _NT), 0.0), x_a, y_k)
    l_rb = each(lambda x, y: jnp.where(low_i, _dot1(x, y, _NT), 0.0), x_r, y_b)
    l_rk = each(lambda x, y: jnp.where(low_i, _dot1(x, y, _NT), 0.0), x_r, y_k)
    t_inv = each(lambda l: eye_n + l, l_ab)
    pw = l_ab
    span = 2
    while span < c:
        pw = each(lambda p: _dot1(p, p), pw)
        t_inv = each(lambda t, p: t + _dot1(t, p), t_inv, pw)
        span *= 2
    resid = each(lambda t, l: (eye_n - t) + _dot3(l, t), t_inv, l_ab)
    t_inv = each(lambda t, e: t + _dot1(t, e), t_inv, resid)
    a_hat = each(lambda t, x: halves(_dot1(t, stack(x))), t_inv, a_t)
    lv = each(_dot1, l_ak, v_st)
    u_hat = each(lambda t, x: halves(_dot1(t, x)), t_inv, lv)
    r_hat = each(lambda x, l, ah: x + halves(_dot1(l, stack(ah))), r_t, l_rb, a_hat)
    y_hat = each(lambda lb, uh, lk, vs: halves(_dot1(lb, stack(uh)) + _dot1(lk, vs)), l_rb, u_hat, l_rk, v_st)
    g_low = each(lambda b, ah: jnp.where(same_head, _dot1(b, ah, _TN), 0.0), b_bar, a_hat)
    z_mat = each(lambda b, uh, kb, v_: jnp.where(same_head, _dot1(b, uh, _TN) + _dot1(kb, v_, _TN), 0.0),
                 b_bar, u_hat, k_bar, v)

    s = s_scr[...]
    y = []
    for i in subs:
        y.append(_dot1(r_hat[i], s) + y_hat[i])
        s = s * p_col[i] + _dot1(g_low[i], s) + z_mat[i]

    inv_hd = 1.0 / hd
    mean = each(lambda x: _dot_exact_rhs(x, seg) * inv_hd, y)
    yc = each(lambda x, m: x - m, y, mean)
    var = each(lambda x: _dot_exact_rhs(x * x, seg) * inv_hd, yc)
    bonus = each(lambda r_, km, v_: _dot_exact_rhs(r_ * km * r_k, seg) * v_, r, kmod, v)
    for i in subs:
        yn = yc[i] * lax.rsqrt(var[i] + RW_GN_EPS) * ln_w + ln_b
        y_ref[rows[i], :] = ((yn + bonus[i]) * g_ref[rows[i], :]).astype(y_ref.dtype)

    s_scr[...] = s

    @pl.when(bi == pl.num_programs(2) - 1)
    def _():
        sout_ref[...] = s


def _rw_recur(r, k, v, wl, al, g, par, s0, *, n_seq, tg, t_valid):
    rows, d = r.shape
    c = min(64, tg)
    n_sub = min(8, tg // c)
    tb = c * n_sub
    n_blocks = tg // tb
    n_pairs = d // LANES
    tile = pl.BlockSpec((tb, LANES), lambda b, p, bi: (b * n_blocks + bi, p))
    state = pl.BlockSpec((None, None, LANES, LANES), lambda b, p, bi: (b, p, 0, 0))
    return pl.pallas_call(
        functools.partial(_rw_recur_kernel, chunk=c, n_sub=n_sub, t_valid=t_valid),
        out_shape=[jax.ShapeDtypeStruct((rows, d), BF16),
                   jax.ShapeDtypeStruct((n_seq, n_pairs, LANES, LANES), F32)],
        grid=(n_seq, n_pairs, n_blocks),
        in_specs=[tile] * 6 + [pl.BlockSpec((8, LANES), lambda b, p, bi: (0, p)), state],
        out_specs=[tile, state],
        scratch_shapes=[pltpu.VMEM((LANES, LANES), F32)],
        compiler_params=_params("parallel", "parallel", "arbitrary"),
        name="rw_recur",
    )(r, k, v, wl, al, g, par, s0)


def _rw_state_to_pairs(wkv):
    b, h, n, _ = wkv.shape
    st = jnp.swapaxes(wkv, -1, -2).reshape(b, h // 2, 2, n, n)
    z = jnp.zeros_like(st[:, :, 0])
    top = jnp.concatenate([st[:, :, 0], z], axis=-1)
    bot = jnp.concatenate([z, st[:, :, 1]], axis=-1)
    return jnp.concatenate([top, bot], axis=-2)


def _rw_pairs_to_state(s):
    b, p, _, _ = s.shape
    n = RW_HEAD_DIM
    h0 = s[:, :, :n, :n]
    h1 = s[:, :, n:, n:]
    st = jnp.stack([h0, h1], axis=2).reshape(b, 2 * p, n, n)
    return jnp.swapaxes(st, -1, -2)


def _rwkv7(h, shift_prev, wkv_prev, w, *, n_seq, tg, t_valid):
    rows, d = h.shape
    h3 = h.reshape(n_seq, tg, d)
    h_prev = jnp.concatenate([shift_prev[:, None, :], h3[:, :-1]], axis=1).reshape(rows, d)
    xr, xw, xk, xv, xa, xg = _rw_mix(h, h_prev, w["rw_mu"])
    r = _matmul(xr, w["rw_w_r"])
    k = _matmul(xk, w["rw_w_k"])
    v = _matmul(xv, w["rw_w_v"])
    wl = _matmul(_matmul(xw, w["rw_w1"], out_dtype=BF16, act="tanh"), w["rw_w2"])
    al = _matmul(_matmul(xa, w["rw_a1"], out_dtype=BF16), w["rw_a2"])
    g = _matmul(_matmul(xg, w["rw_g1"], out_dtype=BF16, act="sigmoid"), w["rw_g2"])
    par = jnp.stack([w["rw_w0"], w["rw_a0"], w["rw_k_k"], w["rw_k_a"], w["rw_r_k"].reshape(-1),
                     w["rw_ln_w"], w["rw_ln_b"], jnp.zeros((d,), F32)])
    y, s_fin = _rw_recur(r, k, v, wl, al, g, par, _rw_state_to_pairs(wkv_prev),
                         n_seq=n_seq, tg=tg, t_valid=t_valid)
    return y, h3[:, t_valid - 1], _rw_pairs_to_state(s_fin)


def _to_heads_kernel(x_ref, o_ref, *, n_heads, hd):
    tm = x_ref.shape[0]
    for h in range(n_heads):
        o_ref[pl.ds(h, tm, stride=n_heads), :] = x_ref[:, h * hd:(h + 1) * hd]


def _to_heads(x, third, n_seq, tg, head_shape):
    rows = x.shape[0]
    hd = head_shape[-1]
    n_heads = math.prod(head_shape[:-1])
    d = n_heads * hd
    tm = min(256, rows)
    out = pl.pallas_call(
        functools.partial(_to_heads_kernel, n_heads=n_heads, hd=hd),
        out_shape=jax.ShapeDtypeStruct((rows * n_heads, hd), x.dtype),
        grid=(rows // tm,),
        in_specs=[pl.BlockSpec((tm, d), lambda i: (i, third))],
        out_specs=pl.BlockSpec((tm * n_heads, hd), lambda i: (i, 0)),
        compiler_params=_params("parallel"),
        name="to_heads",
    )(x)
    return out.reshape((n_seq, tg) + tuple(head_shape))


def _sb_tile(z, mask, later_carry, u_mat):
    lk0 = -_softplus(z)
    lk = lk0 if mask is None else jnp.where(mask, lk0, 0.0)
    later = _dot_exact_rhs(lk, u_mat)
    w = jnp.exp(z + lk0 + later + later_carry)
    if mask is not None:
        w = jnp.where(mask, w, 0.0)
    return w, later_carry + later[:, 0:1] + lk[:, 0:1]


def _suffix_matrix(n):
    j = lax.broadcasted_iota(jnp.int32, (n, n), 0)
    k = lax.broadcasted_iota(jnp.int32, (n, n), 1)
    return jnp.where(j > k, 1.0, 0.0)


def _sb_prompt_kernel(q_ref, k_ref, v_ref, o_ref, *, bq, n_par):
    qi = pl.program_id(2)
    hd = SB_HEAD_DIM
    scale = hd ** -0.5
    heads = range(n_par)
    cols = [slice(h * hd, (h + 1) * hd) for h in heads]
    q = [q_ref[:, c].astype(BF16) for c in cols]
    u_mat = _suffix_matrix(bq)
    row = lax.broadcasted_iota(jnp.int32, (bq, bq), 0)
    col = lax.broadcasted_iota(jnp.int32, (bq, bq), 1)

    def block(kb, state, mask):
        start = pl.multiple_of(kb * bq, bq)
        z = [lax.dot_general(q[h], k_ref[pl.ds(start, bq), cols[h]].astype(BF16), _NT,
                             preferred_element_type=F32) * scale for h in heads]
        tiles = [_sb_tile(z[h], mask, state[h][1], u_mat) for h in heads]
        return tuple(
            (state[h][0] + jnp.dot(tiles[h][0].astype(BF16), v_ref[pl.ds(start, bq), cols[h]].astype(BF16),
                                   preferred_element_type=F32), tiles[h][1]) for h in heads)

    init = tuple((jnp.zeros((bq, hd), F32), jnp.zeros((bq, 1), F32)) for _ in heads)
    state = block(qi, init, col < row)
    state = lax.fori_loop(1, qi + 1, lambda j, st: block(qi - j, st, None), state)
    for h in heads:
        o_ref[:, cols[h]] = state[h][0].astype(o_ref.dtype)


def _sb_prompt(qkv, *, n_seq, tg):
    rows, d3 = qkv.shape
    d = d3 // 3
    hd = SB_HEAD_DIM
    n_heads = d // hd
    n_par = 2 if n_heads % 2 == 0 else 1
    hw = n_par * hd
    n_hb = n_heads // n_par
    bq = min(256, tg)
    nq = tg // bq
    return pl.pallas_call(
        functools.partial(_sb_prompt_kernel, bq=bq, n_par=n_par),
        out_shape=jax.ShapeDtypeStruct((rows, d), BF16),
        grid=(n_seq, n_hb, nq),
        in_specs=[
            pl.BlockSpec((bq, hw), lambda b, h, qi: (b * nq + qi, h)),
            pl.BlockSpec((tg, hw), lambda b, h, qi: (b, n_hb + h)),
            pl.BlockSpec((tg, hw), lambda b, h, qi: (b, 2 * n_hb + h)),
        ],
        out_specs=pl.BlockSpec((bq, hw), lambda b, h, qi: (b * nq + qi, h)),
        compiler_params=_params("parallel", "parallel", "arbitrary"),
        name="sb_prompt",
    )(qkv, qkv, qkv)


def _sb_sample_kernel(pt_ref, q_ref, kn_ref, vn_ref, kc_ref, vc_ref, o_ref, acc_scr, carry_scr,
                      *, n_heads, t_valid, page):
    j = pl.program_id(1)
    hd = SB_HEAD_DIM
    slot = SAMPLE_SLOT
    scale = hd ** -0.5
    n_rows = n_heads * slot

    def tile(get_k, get_v, nk, first):
        q = q_ref[...].astype(BF16)
        z = jnp.concatenate(
            [lax.dot_general(q[:, h * hd:(h + 1) * hd], get_k(h).astype(BF16), _NT, preferred_element_type=F32)
             for h in range(n_heads)], axis=0) * scale
        if first:
            row = lax.broadcasted_iota(jnp.int32, (n_rows, nk), 0)
            col = lax.broadcasted_iota(jnp.int32, (n_rows, nk), 1)
            mask = col < jnp.minimum(row & (slot - 1), t_valid)
            carry = jnp.zeros((n_rows, 1), F32)
        else:
            mask = None
            carry = carry_scr[:, 0:1]
        w, carry = _sb_tile(z, mask, carry, _suffix_matrix(nk))
        carry_scr[...] = jnp.broadcast_to(carry, (n_rows, LANES))
        wb = w.astype(BF16)
        for h in range(n_heads):
            pv = jnp.dot(wb[h * slot:(h + 1) * slot], get_v(h).astype(BF16), preferred_element_type=F32)
            sl = slice(h * hd, (h + 1) * hd)
            acc_scr[:, sl] = pv if first else acc_scr[:, sl] + pv

    @pl.when(j == 0)
    def _():
        tile(lambda h: kn_ref[:, h * hd:(h + 1) * hd], lambda h: vn_ref[:, h * hd:(h + 1) * hd], slot, True)

    @pl.when(j > 0)
    def _():
        tile(lambda h: kc_ref[pl.ds(h, page, stride=n_heads), :],
             lambda h: vc_ref[pl.ds(h, page, stride=n_heads), :], page, False)

    @pl.when(j == pl.num_programs(1) - 1)
    def _():
        o_ref[...] = acc_scr[...].astype(o_ref.dtype)


def _sb_sample(qkv, cache_k, cache_v, page_table, *, t_valid):
    n_seq, n_pages = page_table.shape
    rows, d3 = qkv.shape
    d = d3 // 3
    n_phys, page, n_heads, hd = cache_k.shape
    slot = SAMPLE_SLOT

    ck = cache_k.reshape(n_phys, page * n_heads, hd)
    cv = cache_v.reshape(n_phys, page * n_heads, hd)
    page_spec = pl.BlockSpec((None, page * n_heads, hd), lambda s, j, pt: (pt[s, n_pages - jnp.maximum(j, 1)], 0, 0))

    grid_spec = pltpu.PrefetchScalarGridSpec(
        num_scalar_prefetch=1,
        grid=(n_seq, n_pages + 1),
        in_specs=[
            pl.BlockSpec((slot, d), lambda s, j, pt: (s, 0)),
            pl.BlockSpec((slot, d), lambda s, j, pt: (s, 1)),
            pl.BlockSpec((slot, d), lambda s, j, pt: (s, 2)),
            page_spec,
            page_spec,
        ],
        out_specs=pl.BlockSpec((slot, d), lambda s, j, pt: (s, 0)),
        scratch_shapes=[pltpu.VMEM((slot, d), F32), pltpu.VMEM((n_heads * slot, LANES), F32)],
    )
    return pl.pallas_call(
        functools.partial(_sb_sample_kernel, n_heads=n_heads, t_valid=t_valid, page=page),
        out_shape=jax.ShapeDtypeStruct((rows, d), BF16),
        grid_spec=grid_spec,
        compiler_params=_params("parallel", "arbitrary"),
        name="sb_sample",
    )(page_table, qkv, qkv, qkv, ck, cv)


def _rope_kernel(x_ref, cs_ref, sn_ref, o_ref):
    cs = cs_ref[...]
    sn = sn_ref[...]
    for c in range(x_ref.shape[1] // LANES):
        x = x_ref[:, c * LANES:(c + 1) * LANES]
        o_ref[:, c * LANES:(c + 1) * LANES] = x * cs + pltpu.roll(x, LANES // 2, 1) * sn


def _rope_tables(pos):
    half = DA_HEAD_DIM // 2
    inv_freq = ROPE_THETA ** (-jnp.arange(half, dtype=F32) / half)
    ang = pos.astype(F32)[:, None] * inv_freq[None, :]
    cos, sin = jnp.cos(ang), jnp.sin(ang)
    return jnp.concatenate([cos, cos], axis=1), jnp.concatenate([-sin, sin], axis=1)


def _rope_qk(qkv, pos_rows):
    rows, d3 = qkv.shape
    d2 = d3 // 3 * 2
    cs, sn = _rope_tables(pos_rows)
    tt = cs.shape[0]
    tm = min(256, tt)
    tn = min(1024, d2)
    per = tt // tm
    return pl.pallas_call(
        _rope_kernel,
        out_shape=jax.ShapeDtypeStruct((rows, d2), F32),
        grid=(rows // tm, d2 // tn),
        in_specs=[
            pl.BlockSpec((tm, tn), lambda i, j: (i, j)),
            pl.BlockSpec((tm, LANES), lambda i, j: (i % per, 0)),
            pl.BlockSpec((tm, LANES), lambda i, j: (i % per, 0)),
        ],
        out_specs=pl.BlockSpec((tm, tn), lambda i, j: (i, j)),
        compiler_params=_params("parallel", "parallel"),
        name="rope",
    )(qkv, cs, sn)


def _softmax_tile(s, mask, m_prev, l_prev):
    if mask is not None:
        s = jnp.where(mask, s, -jnp.inf)
    m_new = jnp.maximum(m_prev, jnp.max(s, axis=-1, keepdims=True))
    p = jnp.exp(s - m_new)
    alpha = jnp.exp(m_prev - m_new)
    return p, m_new, alpha, alpha * l_prev + jnp.sum(p, axis=-1, keepdims=True)


def _da_finish(acc1, l1, acc2, l2, lam, subln, lam_init):
    o = acc1 / l1 - lam * (acc2 / l2)
    o = o * lax.rsqrt(jnp.mean(o * o, axis=-1, keepdims=True) + DA_SUBLN_EPS) * subln
    return o * (1.0 - lam_init)


def _da_prompt_kernel(lam_ref, q_ref, k_ref, v_ref, sub_ref, o_ref, *, bq, lam_init):
    qi = pl.program_id(2)
    hd = DA_HEAD_DIM
    scale = hd ** -0.5
    q1 = q_ref[:, :hd].astype(BF16)
    q2 = q_ref[:, hd:].astype(BF16)
    row = lax.broadcasted_iota(jnp.int32, (bq, bq), 0)
    col = lax.broadcasted_iota(jnp.int32, (bq, bq), 1)

    def block(kb, state, mask):
        m1, l1, acc1, m2, l2, acc2 = state
        start = pl.multiple_of(kb * bq, bq)
        k_blk = k_ref[pl.ds(start, bq), :].astype(BF16)
        v_blk = v_ref[pl.ds(start, bq), :].astype(BF16)
        s1 = lax.dot_general(q1, k_blk[:, :hd], _NT, preferred_element_type=F32) * scale
        s2 = lax.dot_general(q2, k_blk[:, hd:], _NT, preferred_element_type=F32) * scale
        p1, m1, a1, l1 = _softmax_tile(s1, mask, m1, l1)
        p2, m2, a2, l2 = _softmax_tile(s2, mask, m2, l2)
        acc1 = a1 * acc1 + jnp.dot(p1.astype(BF16), v_blk, preferred_element_type=F32)
        acc2 = a2 * acc2 + jnp.dot(p2.astype(BF16), v_blk, preferred_element_type=F32)
        return m1, l1, acc1, m2, l2, acc2

    neg = jnp.full((bq, 1), -jnp.inf, F32)
    zero = jnp.zeros((bq, 1), F32)
    zacc = jnp.zeros((bq, 2 * hd), F32)
    state = lax.fori_loop(0, qi, lambda kb, st: block(kb, st, None), (neg, zero, zacc, neg, zero, zacc))
    _, l1, acc1, _, l2, acc2 = block(qi, state, col <= row)
    o_ref[...] = _da_finish(acc1, l1, acc2, l2, lam_ref[0], sub_ref[...], lam_init).astype(o_ref.dtype)


def _da_prompt(qk, qkv, lam, subln, *, n_seq, tg, lam_init):
    rows, d2 = qk.shape
    d = d2 // 2
    hw = 2 * DA_HEAD_DIM
    n_heads = d // hw
    bq = min(256, tg)
    nq = tg // bq
    return pl.pallas_call(
        functools.partial(_da_prompt_kernel, bq=bq, lam_init=lam_init),
        out_shape=jax.ShapeDtypeStruct((rows, d), BF16),
        grid=(n_seq, n_heads, nq),
        in_specs=[
            pl.BlockSpec(memory_space=pltpu.SMEM),
            pl.BlockSpec((bq, hw), lambda b, h, qi: (b * nq + qi, h)),
            pl.BlockSpec((tg, hw), lambda b, h, qi: (b, n_heads + h)),
            pl.BlockSpec((tg, hw), lambda b, h, qi: (b, 2 * n_heads + h)),
            pl.BlockSpec((1, hw), lambda b, h, qi: (0, 0)),
        ],
        out_specs=pl.BlockSpec((bq, hw), lambda b, h, qi: (b * nq + qi, h)),
        compiler_params=_params("parallel", "parallel", "arbitrary"),
        name="da_prompt",
    )(lam, qk, qk, qkv, subln.reshape(1, hw))


def _da_sample_kernel(pt_ref, q_ref, kn_ref, vn_ref, kc_ref, vlo_ref, vhi_ref, lam_ref, sub_ref, o_ref,
                      m_scr, l_scr, acc_scr, *, n_heads, t_valid, page, lam_init):
    j = pl.program_id(1)
    hd = DA_HEAD_DIM
    hw = 2 * hd
    slot = SAMPLE_SLOT
    scale = hd ** -0.5
    n_rows = 2 * n_heads * slot

    def tile(get_k, get_v, nk, first):
        q = q_ref[...].astype(BF16)
        s = jnp.concatenate(
            [lax.dot_general(q[:, i * hd:(i + 1) * hd], get_k(i // 2, i % 2).astype(BF16), _NT,
                             preferred_element_type=F32) for i in range(2 * n_heads)], axis=0) * scale
        if first:
            row = lax.broadcasted_iota(jnp.int32, (n_rows, nk), 0)
            col = lax.broadcasted_iota(jnp.int32, (n_rows, nk), 1)
            mask = col <= jnp.minimum(row & (slot - 1), t_valid - 1)
            m_prev = jnp.full((n_rows, 1), -jnp.inf, F32)
            l_prev = jnp.zeros((n_rows, 1), F32)
        else:
            mask = None
            m_prev = m_scr[:, 0:1]
            l_prev = l_scr[:, 0:1]
        p, m_new, alpha, l_new = _softmax_tile(s, mask, m_prev, l_prev)
        m_scr[...] = jnp.broadcast_to(m_new, (n_rows, LANES))
        l_scr[...] = jnp.broadcast_to(l_new, (n_rows, LANES))
        pb = p.astype(BF16)
        for h in range(n_heads):
            vb = get_v(h).astype(BF16)
            sl = slice(h * hw, (h + 1) * hw)
            for c in range(2):
                rows = slice((2 * h + c) * slot, (2 * h + c + 1) * slot)
                pv = jnp.dot(pb[rows], vb, preferred_element_type=F32)
                acc_scr[c, :, sl] = pv if first else alpha[rows] * acc_scr[c, :, sl] + pv

    @pl.when(j == 0)
    def _():
        tile(lambda h, c: kn_ref[:, (2 * h + c) * hd:(2 * h + c + 1) * hd],
             lambda h: vn_ref[:, h * hw:(h + 1) * hw], slot, True)

    @pl.when(j > 0)
    def _():
        tile(lambda h, c: kc_ref[pl.ds(2 * h + c, page, stride=2 * n_heads), :],
             lambda h: jnp.concatenate([vlo_ref[pl.ds(h, page, stride=n_heads), :],
                                        vhi_ref[pl.ds(h, page, stride=n_heads), :]], axis=1), page, False)

    @pl.when(j == pl.num_programs(1) - 1)
    def _():
        l_all = l_scr[:, 0:1]
        for h in range(n_heads):
            sl = slice(h * hw, (h + 1) * hw)
            r1 = slice(2 * h * slot, (2 * h + 1) * slot)
            r2 = slice((2 * h + 1) * slot, (2 * h + 2) * slot)
            o = _da_finish(acc_scr[0, :, sl], l_all[r1], acc_scr[1, :, sl], l_all[r2],
                           lam_ref[0], sub_ref[...], lam_init)
            o_ref[:, sl] = o.astype(o_ref.dtype)


def _da_sample(qk, qkv, cache_k, cache_v, page_table, lam, subln, *, t_valid, lam_init):
    rows, d2 = qk.shape
    d = d2 // 2
    hd = DA_HEAD_DIM
    hw = 2 * hd
    n_heads = d // hw
    n_seq, n_pages = page_table.shape
    n_phys, page = cache_k.shape[:2]
    ck = cache_k.reshape(n_phys, page * 2 * n_heads, hd)
    cv = cache_v.reshape(n_phys, page * n_heads, hw)
    slot = SAMPLE_SLOT

    def page_map(half):
        return lambda s, j, pt: (pt[s, n_pages - jnp.maximum(j, 1)], 0, half)

    grid_spec = pltpu.PrefetchScalarGridSpec(
        num_scalar_prefetch=1,
        grid=(n_seq, n_pages + 1),
        in_specs=[
            pl.BlockSpec((slot, d), lambda s, j, pt: (s, 0)),
            pl.BlockSpec((slot, d), lambda s, j, pt: (s, 1)),
            pl.BlockSpec((slot, d), lambda s, j, pt: (s, 2)),
            pl.BlockSpec((None, page * 2 * n_heads, hd), page_map(0)),
            pl.BlockSpec((None, page * n_heads, hd), page_map(0)),
            pl.BlockSpec((None, page * n_heads, hd), page_map(1)),
            pl.BlockSpec(memory_space=pltpu.SMEM),
            pl.BlockSpec((1, hw), lambda s, j, pt: (0, 0)),
        ],
        out_specs=pl.BlockSpec((slot, d), lambda s, j, pt: (s, 0)),
        scratch_shapes=[pltpu.VMEM((2 * n_heads * slot, LANES), F32),
                        pltpu.VMEM((2 * n_heads * slot, LANES), F32),
                        pltpu.VMEM((2, slot, d), F32)],
    )
    return pl.pallas_call(
        functools.partial(_da_sample_kernel, n_heads=n_heads, t_valid=t_valid, page=page, lam_init=lam_init),
        out_shape=jax.ShapeDtypeStruct((rows, d), BF16),
        grid_spec=grid_spec,
        compiler_params=_params("parallel", "arbitrary"),
        name="da_sample",
    )(page_table, qk, qk, qkv, ck, cv, cv, lam, subln.reshape(1, hw))


def _gla_kernel(q_ref, k_ref, v_ref, gate_ref, lo_ref, ba_ref, nw_ref, s0_ref, o_ref, sout_ref, s_scr,
                *, tb, c, tg, t_valid, dk):
    ci = pl.program_id(2)

    @pl.when(ci == 0)
    def _():
        s_scr[...] = s0_ref[...]

    ti = lax.broadcasted_iota(jnp.int32, (c, c), 0)
    si = lax.broadcasted_iota(jnp.int32, (c, c), 1)
    tri = jnp.where(ti >= si, 1.0, 0.0)
    row_id = lax.broadcasted_iota(jnp.int32, (c, 1), 0)
    ones_cl = jnp.ones((c, LANES), F32)
    ba = ba_ref[...]
    nw = nw_ref[...]
    for sub in range(tb // c):
        rows = slice(sub * c, (sub + 1) * c)
        q = q_ref[rows, :] * (dk ** -0.5)
        k = k_ref[rows, :]
        v = v_ref[rows, :]
        la = -_softplus(-(lo_ref[rows, :] + ba)) / GLA_TAU
        if t_valid < tg:
            valid = (ci * tb + sub * c + row_id) < t_valid
            la = jnp.where(valid, la, 0.0)
            k = jnp.where(valid, k, 0.0)
        cum = _dot_exact_lhs(tri, la)
        cum_end = cum[c - 1:c, :]
        s = s_scr[...]
        o = _dot1(q * jnp.exp(cum), s)
        att = jnp.zeros((c, c), F32)
        for j in range(c):
            e = jnp.exp(jnp.where(row_id >= j, cum - cum[j:j + 1, :], -jnp.inf))
            p = jnp.sum(q * k[j:j + 1, :] * e, axis=-1, keepdims=True)
            att = jnp.where(si == j, p, att)
        o = o + _dot1(att, v)
        dec = jnp.exp(_dot_exact_rhs(la, ones_cl, _TN)[:, 0:1])
        s_scr[...] = s * dec + _dot1(k * jnp.exp(cum_end - cum), v, _TN)
        on = o * lax.rsqrt(jnp.mean(o * o, axis=-1, keepdims=True) + GLA_NORM_EPS) * nw
        g = gate_ref[rows, :]
        o_ref[rows, :] = (on * (g * _sigmoid(g))).astype(o_ref.dtype)

    @pl.when(ci == pl.num_programs(2) - 1)
    def _():
        sout_ref[...] = s_scr[...]


def _gla(proj, lo, ba, norm_w, s0, *, n_seq, tg, t_valid):
    rows = proj.shape[0]
    key = lo.shape[1]
    d = (proj.shape[1] - 2 * key) // 2
    nh = GLA_HEADS
    dk, dv = key // nh, d // nh
    tb = min(64, tg)
    c = min(16, tb)
    nc = tg // tb
    kv = key // dv
    return pl.pallas_call(
        functools.partial(_gla_kernel, tb=tb, c=c, tg=tg, t_valid=t_valid, dk=dk),
        out_shape=[jax.ShapeDtypeStruct((rows, d), BF16), jax.ShapeDtypeStruct((n_seq, nh, dk, dv), F32)],
        grid=(n_seq, nh, nc),
        in_specs=[
            pl.BlockSpec((tb, dk), lambda b, h, ci: (b * nc + ci, h)),
            pl.BlockSpec((tb, dk), lambda b, h, ci: (b * nc + ci, nh + h)),
            pl.BlockSpec((tb, dv), lambda b, h, ci: (b * nc + ci, 2 * kv + h)),
            pl.BlockSpec((tb, dv), lambda b, h, ci: (b * nc + ci, 2 * kv + nh + h)),
            pl.BlockSpec((tb, dk), lambda b, h, ci: (b * nc + ci, h)),
            pl.BlockSpec((1, dk), lambda b, h, ci: (0, h)),
            pl.BlockSpec((1, dv), lambda b, h, ci: (0, 0)),
            pl.BlockSpec((None, None, dk, dv), lambda b, h, ci: (b, h, 0, 0)),
        ],
        out_specs=[pl.BlockSpec((tb, dv), lambda b, h, ci: (b * nc + ci, h)),
                   pl.BlockSpec((None, None, dk, dv), lambda b, h, ci: (b, h, 0, 0))],
        scratch_shapes=[pltpu.VMEM((dk, dv), F32)],
        compiler_params=_params("parallel", "parallel", "arbitrary"),
        name="gla",
    )(proj, proj, proj, proj, lo, ba.reshape(1, key), norm_w.reshape(1, dv), s0)


MOE_BLOCK_ROWS = 256


def _moe_route(logits, n_groups, n_experts):
    g_logits = logits[:, :n_groups]
    grp = jnp.argmax(g_logits, axis=-1)
    g_w = jnp.take_along_axis(jax.nn.softmax(g_logits, axis=-1), grp[:, None], axis=-1)
    e_logits = logits[:, n_groups:n_groups + n_groups * n_experts].reshape(-1, n_groups, n_experts)
    e_logits = jnp.take_along_axis(e_logits, grp[:, None, None], axis=1)[:, 0]
    top_val, top_idx = lax.top_k(e_logits, MOE_TOPK)
    gate = g_w * jax.nn.softmax(top_val, axis=-1)
    return (grp[:, None] * n_experts + top_idx).astype(jnp.int32), gate


def _moe_plan(expert_id, n_exp):
    tm = MOE_BLOCK_ROWS
    n_tok, top_k = expert_id.shape
    n_assign = n_tok * top_k
    e_flat = expert_id.reshape(-1)
    order = jnp.argsort(e_flat)
    e_sorted = e_flat[order]
    counts = jnp.bincount(e_flat, length=n_exp)
    padded = (counts + tm - 1) // tm * tm
    pad_end = jnp.cumsum(padded)
    rank = jnp.arange(n_assign) - (jnp.cumsum(counts) - counts)[e_sorted]
    dest = ((pad_end - padded)[e_sorted] + rank).astype(jnp.int32)
    n_blocks = -(-n_assign // tm) + n_exp
    src_tok = jnp.zeros((n_blocks * tm,), jnp.int32).at[dest].set((order // top_k).astype(jnp.int32))
    pos = jnp.zeros((n_assign,), jnp.int32).at[order].set(dest).reshape(n_tok, top_k)
    block_expert = jnp.minimum(jnp.searchsorted(pad_end, jnp.arange(n_blocks) * tm, side="right"), n_exp - 1)
    n_used = (pad_end[-1] // tm).astype(jnp.int32).reshape(1)
    return src_tok, pos, block_expert.astype(jnp.int32), n_used


def _new_expert(be_ref, b):
    return jnp.logical_or(b == 0, be_ref[b] != be_ref[jnp.maximum(b - 1, 0)])


def _moe_up_kernel(be_ref, nu_ref, x_ref, wg_ref, wu_ref, h_ref, wg_bf, wu_bf):
    b = pl.program_id(0)
    used = b < nu_ref[0]

    @pl.when(jnp.logical_and(used, _new_expert(be_ref, b)))
    def _():
        wg_bf[...] = wg_ref[...].astype(BF16)
        wu_bf[...] = wu_ref[...].astype(BF16)

    @pl.when(used)
    def _():
        x = x_ref[...]
        g = jnp.dot(x, wg_bf[...], preferred_element_type=F32)
        u = jnp.dot(x, wu_bf[...], preferred_element_type=F32)
        h_ref[...] = (g * _sigmoid(g) * u).astype(h_ref.dtype)

    @pl.when(jnp.logical_not(used))
    def _():
        h_ref[...] = jnp.zeros_like(h_ref)


def _moe_down_kernel(be_ref, nu_ref, h_ref, wd_ref, o_ref, wd_bf):
    b = pl.program_id(0)
    used = b < nu_ref[0]

    @pl.when(jnp.logical_and(used, _new_expert(be_ref, b)))
    def _():
        wd_bf[...] = wd_ref[...].astype(BF16)

    @pl.when(used)
    def _():
        o_ref[...] = jnp.dot(h_ref[...], wd_bf[...], preferred_element_type=F32)

    @pl.when(jnp.logical_not(used))
    def _():
        o_ref[...] = jnp.zeros_like(o_ref)


def _moe_experts(rows, block_expert, n_used, w_gate, w_up, w_down):
    tm = MOE_BLOCK_ROWS
    n_rows, d = rows.shape
    f = w_gate.shape[2]
    n_blocks = n_rows // tm
    up_spec = pltpu.PrefetchScalarGridSpec(
        num_scalar_prefetch=2,
        grid=(n_blocks,),
        in_specs=[
            pl.BlockSpec((tm, d), lambda b, be, nu: (b, 0)),
            pl.BlockSpec((None, d, f), lambda b, be, nu: (be[b], 0, 0)),
            pl.BlockSpec((None, d, f), lambda b, be, nu: (be[b], 0, 0)),
        ],
        out_specs=pl.BlockSpec((tm, f), lambda b, be, nu: (b, 0)),
        scratch_shapes=[pltpu.VMEM((d, f), BF16), pltpu.VMEM((d, f), BF16)],
    )
    h = pl.pallas_call(
        _moe_up_kernel,
        out_shape=jax.ShapeDtypeStruct((n_rows, f), BF16),
        grid_spec=up_spec,
        compiler_params=_params("arbitrary"),
        name="moe_up",
    )(block_expert, n_used, rows, w_gate, w_up)
    down_spec = pltpu.PrefetchScalarGridSpec(
        num_scalar_prefetch=2,
        grid=(n_blocks,),
        in_specs=[
            pl.BlockSpec((tm, f), lambda b, be, nu: (b, 0)),
            pl.BlockSpec((None, f, d), lambda b, be, nu: (be[b], 0, 0)),
        ],
        out_specs=pl.BlockSpec((tm, d), lambda b, be, nu: (b, 0)),
        scratch_shapes=[pltpu.VMEM((f, d), BF16)],
    )
    return pl.pallas_call(
        _moe_down_kernel,
        out_shape=jax.ShapeDtypeStruct((n_rows, d), F32),
        grid_spec=down_spec,
        compiler_params=_params("arbitrary"),
        name="moe_down",
    )(block_expert, n_used, h, w_down)


def _moe_combine_kernel(x_ref, o0_ref, o1_ref, g_ref, m_ref, y_ref):
    g = g_ref[...]
    y = o0_ref[...] * g[:, 0:1] + o1_ref[...] * g[:, 1:2]
    y_ref[...] = x_ref[...] + m_ref[...] * y


def _moe_combine(x, o0, o1, gates, gmod, *, tg):
    rows, d = x.shape
    tm = min(256, tg) if gmod.shape[1] == 1 else rows
    tile = pl.BlockSpec((tm, d), lambda i: (i, 0))
    return pl.pallas_call(
        _moe_combine_kernel,
        out_shape=jax.ShapeDtypeStruct((rows, d), F32),
        grid=(rows // tm,),
        in_specs=[tile, tile, tile, pl.BlockSpec((tm, MOE_TOPK), lambda i: (i, 0)), _mod_spec(gmod, tm, tg, d)],
        out_specs=tile,
        compiler_params=_params("parallel"),
        name="moe_combine",
    )(x, o0, o1, gates, gmod)


def _moe_layer(groups, xs, hs, logits, gmods, w_group, w_gate, w_up, w_down):
    n_groups, n_experts = w_group.shape[1], w_gate.shape[0] // w_group.shape[1]
    d = xs[0].shape[1]
    real_h, real_l = [], []
    for grp, h, lg in zip(groups, hs, logits):
        n_seq, tg, tv = grp
        real_h.append(h.reshape(n_seq, tg, d)[:, :tv].reshape(n_seq * tv, d))
        real_l.append(lg.reshape(n_seq, tg, -1)[:, :tv].reshape(n_seq * tv, -1))
    h_all = jnp.concatenate(real_h, axis=0)
    expert_id, gate = _moe_route(jnp.concatenate(real_l, axis=0), n_groups, n_experts)
    src_tok, pos, block_expert, n_used = _moe_plan(expert_id, n_groups * n_experts)
    out = _moe_experts(h_all[src_tok], block_expert, n_used, w_gate, w_up, w_down)
    o0, o1 = out[pos[:, 0]], out[pos[:, 1]]
    new_xs = []
    start = 0
    for grp, x, gmod in zip(groups, xs, gmods):
        n_seq, tg, tv = grp
        n = n_seq * tv
        parts = []
        for a in (o0[start:start + n], o1[start:start + n], gate[start:start + n]):
            a = a.reshape(n_seq, tv, -1)
            if tv < tg:
                a = jnp.pad(a, ((0, 0), (0, tg - tv), (0, 0)))
            parts.append(a.reshape(n_seq * tg, -1))
        new_xs.append(_moe_combine(x, parts[0], parts[1], parts[2], gmod, tg=tg))
        start += n
    return new_xs


def kernel(x_prompt, x_sample, c_prompt, c_sample, state_rwkv_shift, state_rwkv_wkv, cache_sb_k, cache_sb_v, cache_da_k, cache_da_v, state_gla, page_table, ada_w, ada_b, norm_mix, norm_ffn, norm_final, rw_mu, rw_w_r, rw_w_k, rw_w_v, rw_w_o, rw_w0, rw_w1, rw_w2, rw_a0, rw_a1, rw_a2, rw_g1, rw_g2, rw_k_k, rw_k_a, rw_r_k, rw_ln_w, rw_ln_b, sb_w_in, sb_w_out, da_w_in, da_w_out, da_lambda, da_subln, gla_w_in, gla_wa1, gla_wa2, gla_ba, gla_norm, gla_w_out, moe_w_group, moe_w_expert, moe_w_gate, moe_w_up, moe_w_down):
    n_p, t_p, d = x_prompt.shape
    n_s, t_s, _ = x_sample.shape
    depth = ada_w.shape[0]
    slot = SAMPLE_SLOT
    n_pages, page = page_table.shape[1], cache_sb_k.shape[1]
    past_len = n_pages * page
    groups = [(n_p, t_p, t_p), (n_s, slot, t_s)]

    n_c = n_p + n_s
    c_all = jnp.concatenate([c_prompt, c_sample], axis=0)
    c_all = jnp.pad(c_all, ((0, -n_c % SUBLANES), (0, 0)))
    mod = _adaln(c_all, ada_w, ada_b)

    xs = [x_prompt.reshape(n_p * t_p, d),
          jnp.pad(x_sample, ((0, 0), (0, slot - t_s), (0, 0))).reshape(n_s * slot, d)]
    rw = dict(rw_mu=rw_mu, rw_w0=rw_w0, rw_a0=rw_a0, rw_k_k=rw_k_k, rw_k_a=rw_k_a, rw_r_k=rw_r_k,
              rw_ln_w=rw_ln_w, rw_ln_b=rw_ln_b)
    for name, val in (("rw_w_r", rw_w_r), ("rw_w_k", rw_w_k), ("rw_w_v", rw_w_v), ("rw_w1", rw_w1),
                      ("rw_w2", rw_w2), ("rw_a1", rw_a1), ("rw_a2", rw_a2), ("rw_g1", rw_g1), ("rw_g2", rw_g2)):
        rw[name] = val.astype(BF16)
    pos = [jnp.arange(t_p, dtype=jnp.int32), jnp.tile(past_len + jnp.arange(slot, dtype=jnp.int32), n_s)]
    rw_state = [(jnp.zeros((n_p, d), F32), jnp.zeros((n_p, d // RW_HEAD_DIM, RW_HEAD_DIM, RW_HEAD_DIM), F32)),
                (state_rwkv_shift, state_rwkv_wkv)]
    gla_state = [jnp.zeros((n_p,) + state_gla.shape[1:], F32), state_gla]
    outs = [dict(), dict()]

    for i in range(depth):
        mods = []
        for gi, (n_seq, tg, tv) in enumerate(groups):
            rows = mod[i, :n_p] if gi == 0 else mod[i, n_p:n_c]
            six = [rows[:, j * d:(j + 1) * d] for j in range(6)]
            if gi == 0:
                mods.append([m[:, None, :] for m in six])
            else:
                mods.append([jnp.repeat(m, tg, axis=0)[None] for m in six])
        kind = i % 4
        for gi, (n_seq, tg, tv) in enumerate(groups):
            m = mods[gi]
            x = xs[gi]
            o = outs[gi]

            def rows_of(a, width_shape):
                return a.reshape((n_seq, tg) + width_shape)[:, :tv]

            if kind == 0:
                (h,) = _norm_mod(x, norm_mix[i], m[0], m[1], tg=tg, out_dtypes=(F32,))
                y, o["rw_shift"], o["rw_wkv"] = _rwkv7(h, rw_state[gi][0], rw_state[gi][1], rw,
                                                       n_seq=n_seq, tg=tg, t_valid=tv)
                w_out = rw_w_o
            elif kind == 1:
                (h,) = _norm_mod(x, norm_mix[i], m[0], m[1], tg=tg, out_dtypes=(BF16,))
                qkv = _matmul(h, sb_w_in.astype(BF16))
                n_h = d // SB_HEAD_DIM
                o["sb_k"] = _to_heads(qkv, 1, n_seq, tg, (n_h, SB_HEAD_DIM))[:, :tv]
                o["sb_v"] = _to_heads(qkv, 2, n_seq, tg, (n_h, SB_HEAD_DIM))[:, :tv]
                if gi == 0:
                    y = _sb_prompt(qkv, n_seq=n_seq, tg=tg)
                else:
                    y = _sb_sample(qkv, cache_sb_k, cache_sb_v, page_table, t_valid=tv)
                w_out = sb_w_out
            elif kind == 2:
                lam_init = 0.8 - 0.6 * math.exp(-0.3 * i)
                lf = da_lambda.astype(F32)
                lam = (jnp.exp(jnp.sum(lf[0] * lf[1])) - jnp.exp(jnp.sum(lf[2] * lf[3])) + lam_init).reshape(1)
                (h,) = _norm_mod(x, norm_mix[i], m[0], m[1], tg=tg, out_dtypes=(BF16,))
                qkv = _matmul(h, da_w_in.astype(BF16))
                qk = _rope_qk(qkv, pos[gi])
                n_h = d // (2 * DA_HEAD_DIM)
                o["da_k"] = _to_heads(qk, 1, n_seq, tg, (n_h, 2, DA_HEAD_DIM))[:, :tv]
                o["da_v"] = rows_of(qkv[:, 2 * d:], (n_h, 2 * DA_HEAD_DIM))
                if gi == 0:
                    y = _da_prompt(qk, qkv, lam, da_subln, n_seq=n_seq, tg=tg, lam_init=lam_init)
                else:
                    y = _da_sample(qk, qkv, cache_da_k, cache_da_v, page_table, lam, da_subln,
                                   t_valid=tv, lam_init=lam_init)
                w_out = da_w_out
            else:
                (h,) = _norm_mod(x, norm_mix[i], m[0], m[1], tg=tg, out_dtypes=(BF16,))
                proj = _matmul(h, gla_w_in.astype(BF16))
                lo = _matmul(_matmul(h, gla_wa1.astype(BF16), out_dtype=BF16), gla_wa2.astype(BF16))
                y, o["gla"] = _gla(proj, lo, gla_ba, gla_norm, gla_state[gi], n_seq=n_seq, tg=tg, t_valid=tv)
                w_out = gla_w_out
            xs[gi] = _matmul(y, w_out.astype(BF16), resid=x, gate=m[2], tg=tg)

        n_g, n_e = moe_w_group.shape[2], moe_w_expert.shape[3]
        router_w = jnp.concatenate([moe_w_group[i]] + [moe_w_expert[i, g] for g in range(n_g)], axis=1)
        router_w = jnp.pad(router_w, ((0, 0), (0, -router_w.shape[1] % LANES)))
        hs, logits = [], []
        for gi, (n_seq, tg, tv) in enumerate(groups):
            m = mods[gi]
            h, lg = _norm_mod(xs[gi], norm_ffn[i], m[3], m[4], tg=tg, out_dtypes=(BF16,), router_w=router_w)
            hs.append(h)
            logits.append(lg)
        xs = _moe_layer(groups, xs, hs, logits, [mods[0][5], mods[1][5]], moe_w_group[i],
                        moe_w_gate[i], moe_w_up[i], moe_w_down[i])

    ys = []
    for gi, (n_seq, tg, tv) in enumerate(groups):
        (y,) = _norm_mod(xs[gi], norm_final, None, None, tg=tg, out_dtypes=(F32,))
        ys.append(y.reshape(n_seq, tg, d)[:, :tv])
    names = ("rw_shift", "rw_wkv", "sb_k", "sb_v", "da_k", "da_v", "gla")
    return (ys[0], ys[1]) + tuple(outs[0][n] for n in names) + tuple(outs[1][n] for n in names)
```

```python
import functools
import math

import jax
import jax.numpy as jnp
from jax import lax
from jax.experimental import pallas as pl
from jax.experimental.pallas import tpu as pltpu

F32 = jnp.float32
BF16 = jnp.bfloat16

LANES = 128
SUBLANES = 8
VMEM_LIMIT_BYTES = 56 * 1024 * 1024

NORM_EPS = 1e-6
ROPE_THETA = 10000.0
RW_HEAD_DIM = 64
RW_GN_EPS = 64e-5
SB_HEAD_DIM = 128
DA_HEAD_DIM = 128
DA_SUBLN_EPS = 1e-5
GLA_HEADS = 4
GLA_TAU = 16.0
GLA_NORM_EPS = 1e-5
MOE_TOPK = 2
SAMPLE_SLOT = 8


def _params(*semantics):
    return pltpu.CompilerParams(dimension_semantics=semantics, vmem_limit_bytes=VMEM_LIMIT_BYTES)


def _split3(x):
    hi = x.astype(BF16)
    r1 = x - hi.astype(F32)
    mid = r1.astype(BF16)
    lo = (r1 - mid.astype(F32)).astype(BF16)
    return hi, mid, lo


_NN = (((1,), (0,)), ((), ()))
_NT = (((1,), (1,)), ((), ()))
_TN = (((0,), (0,)), ((), ()))


def _dot1(a, b, dims=_NN):
    return lax.dot_general(a.astype(BF16), b.astype(BF16), dims, preferred_element_type=F32)


def _dot_exact_rhs(a, b_exact, dims=_NN):
    hi, mid, lo = _split3(a)
    b = b_exact.astype(BF16)
    return (lax.dot_general(hi, b, dims, preferred_element_type=F32)
            + lax.dot_general(mid, b, dims, preferred_element_type=F32)
            + lax.dot_general(lo, b, dims, preferred_element_type=F32))


def _dot_sum16(a, b_exact):
    hi = a.astype(BF16)
    mid = (a - hi.astype(F32)).astype(BF16)
    b = b_exact.astype(BF16)
    return jnp.dot(hi, b, preferred_element_type=F32) + jnp.dot(mid, b, preferred_element_type=F32)


def _dot_exact_lhs(a_exact, b, dims=_NN):
    hi, mid, lo = _split3(b)
    a = a_exact.astype(BF16)
    return (lax.dot_general(a, hi, dims, preferred_element_type=F32)
            + lax.dot_general(a, mid, dims, preferred_element_type=F32)
            + lax.dot_general(a, lo, dims, preferred_element_type=F32))


def _dot3(a, b, dims=_NN):
    ah, am, _ = _split3(a)
    bh, bm, _ = _split3(b)
    d = functools.partial(lax.dot_general, dimension_numbers=dims, preferred_element_type=F32)
    return d(ah, bh) + (d(ah, bm) + d(am, bh))


def _softplus(x):
    return jnp.maximum(x, 0.0) + jnp.log(1.0 + jnp.exp(-jnp.abs(x)))


def _sigmoid(x):
    return 1.0 / (1.0 + jnp.exp(-x))


def _adaln_kernel(c_ref, w_ref, b_ref, o_ref):
    c = c_ref[...]
    s = c * _sigmoid(c)
    o_ref[...] = _dot3(s, w_ref[...]) + b_ref[...]


def _adaln(c_all, ada_w, ada_b):
    n_layers, d, n = ada_w.shape
    rows = c_all.shape[0]
    tn = min(512, n)
    return pl.pallas_call(
        _adaln_kernel,
        out_shape=jax.ShapeDtypeStruct((n_layers, rows, n), F32),
        grid=(n_layers, n // tn),
        in_specs=[
            pl.BlockSpec((rows, d), lambda l, j: (0, 0)),
            pl.BlockSpec((None, d, tn), lambda l, j: (l, 0, j)),
            pl.BlockSpec((None, 1, tn), lambda l, j: (l, 0, j)),
        ],
        out_specs=pl.BlockSpec((None, rows, tn), lambda l, j: (l, 0, j)),
        compiler_params=_params("parallel", "parallel"),
        name="adaln",
    )(c_all, ada_w, ada_b.reshape(n_layers, 1, n))


def _norm_mod_kernel(*refs, eps, modulate, n_out, router):
    x_ref, w_ref = refs[:2]
    pos = 2
    x = x_ref[...]
    y = x * lax.rsqrt(jnp.mean(x * x, axis=-1, keepdims=True) + eps) * w_ref[...]
    if modulate:
        shift_ref, scale_ref = refs[pos:pos + 2]
        pos += 2
        y = y * (1.0 + scale_ref[...]) + shift_ref[...]
    if router:
        wr_ref = refs[pos]
        pos += 1
    outs = refs[pos:]
    for o in outs[:n_out]:
        o[...] = y.astype(o.dtype)
    if router:
        outs[n_out][...] = _dot3(y, wr_ref[...])


def _mod_spec(mod, tm, tg, width):
    g, gr, _ = mod.shape
    if gr == 1:
        per = tg // tm
        return pl.BlockSpec((None, 1, width), lambda i, *_: (i // per, 0, 0))
    assert g == 1 and gr == tm
    return pl.BlockSpec((None, gr, width), lambda i, *_: (0, 0, 0))


def _norm_mod(x, w, shift, scale, *, tg, out_dtypes, eps=NORM_EPS, router_w=None):
    rows, d = x.shape
    tm = min(256, tg) if shift is None or shift.shape[1] == 1 else rows
    in_specs = [pl.BlockSpec((tm, d), lambda i: (i, 0)), pl.BlockSpec((1, d), lambda i: (0, 0))]
    args = [x, w.reshape(1, d)]
    if shift is not None:
        in_specs += [_mod_spec(shift, tm, tg, d), _mod_spec(scale, tm, tg, d)]
        args += [shift, scale]
    out_shape = [jax.ShapeDtypeStruct((rows, d), dt) for dt in out_dtypes]
    out_specs = [pl.BlockSpec((tm, d), lambda i: (i, 0)) for _ in out_dtypes]
    if router_w is not None:
        nr = router_w.shape[1]
        in_specs.append(pl.BlockSpec((d, nr), lambda i: (0, 0)))
        args.append(router_w)
        out_shape.append(jax.ShapeDtypeStruct((rows, nr), F32))
        out_specs.append(pl.BlockSpec((tm, nr), lambda i: (i, 0)))
    return pl.pallas_call(
        functools.partial(_norm_mod_kernel, eps=eps, modulate=shift is not None,
                          n_out=len(out_dtypes), router=router_w is not None),
        out_shape=out_shape,
        grid=(rows // tm,),
        in_specs=in_specs,
        out_specs=out_specs,
        compiler_params=_params("parallel"),
        name="norm_mod",
    )(*args)


def _matmul_kernel(*refs, act, resid):
    a_ref, b_ref = refs[:2]
    o_ref = refs[-1]
    acc = jnp.dot(a_ref[...], b_ref[...], preferred_element_type=F32)
    if act == "tanh":
        acc = jnp.tanh(acc)
    elif act == "sigmoid":
        acc = _sigmoid(acc)
    if resid:
        x_ref, g_ref = refs[2:4]
        acc = x_ref[...] + g_ref[...] * acc
    o_ref[...] = acc.astype(o_ref.dtype)


def _matmul(a, b, *, out_dtype=F32, act=None, resid=None, gate=None, tg=None):
    m, k = a.shape
    n = b.shape[1]
    tm = min(1024, m)
    if gate is not None:
        tm = min(tm, tg) if gate.shape[1] == 1 else m
    tn = min(512, n)
    in_specs = [pl.BlockSpec((tm, k), lambda i, j: (i, 0)), pl.BlockSpec((k, tn), lambda i, j: (0, j))]
    args = [a, b]
    if resid is not None:
        g, gr, _ = gate.shape
        in_specs.append(pl.BlockSpec((tm, tn), lambda i, j: (i, j)))
        if gr == 1:
            per = tg // tm
            in_specs.append(pl.BlockSpec((None, 1, tn), lambda i, j: (i // per, 0, j)))
        else:
            in_specs.append(pl.BlockSpec((None, gr, tn), lambda i, j: (0, 0, j)))
        args += [resid, gate]
    return pl.pallas_call(
        functools.partial(_matmul_kernel, act=act, resid=resid is not None),
        out_shape=jax.ShapeDtypeStruct((m, n), out_dtype),
        grid=(m // tm, n // tn),
        in_specs=in_specs,
        out_specs=pl.BlockSpec((tm, tn), lambda i, j: (i, j)),
        compiler_params=_params("parallel", "parallel"),
        name="matmul",
    )(*args)


def _rw_mix_kernel(h_ref, hp_ref, mu_ref, *o_refs):
    h = h_ref[...]
    xx = hp_ref[...] - h
    for i, o in enumerate(o_refs):
        o[...] = (h + xx * mu_ref[i:i + 1, :]).astype(o.dtype)


def _rw_mix(h, h_prev, mu):
    rows, d = h.shape
    tm = min(256, rows)
    n = mu.shape[0]
    spec = pl.BlockSpec((tm, d), lambda i: (i, 0))
    return pl.pallas_call(
        _rw_mix_kernel,
        out_shape=[jax.ShapeDtypeStruct((rows, d), BF16)] * n,
        grid=(rows // tm,),
        in_specs=[spec, spec, pl.BlockSpec((n, d), lambda i: (0, 0))],
        out_specs=[spec] * n,
        compiler_params=_params("parallel"),
        name="rw_mix",
    )(h, h_prev, mu)


def _rw_recur_kernel(r_ref, k_ref, v_ref, wl_ref, al_ref, g_ref, par_ref, s0_ref,
                     y_ref, sout_ref, s_scr, *, chunk, n_sub, t_valid, precise):
    c = chunk
    n2 = 2 * c
    bi = pl.program_id(2)
    hd = RW_HEAD_DIM

    @pl.when(bi == 0)
    def _():
        s_scr[...] = s0_ref[...]

    par = par_ref[...]
    w0, a0, k_k, k_a, r_k, ln_w, ln_b = (par[i:i + 1, :] for i in range(7))
    lane = lax.broadcasted_iota(jnp.int32, (1, LANES), 1)
    head0 = lane < hd
    lane_r = lax.broadcasted_iota(jnp.int32, (LANES, LANES), 0)
    lane_c = lax.broadcasted_iota(jnp.int32, (LANES, LANES), 1)
    same_head = (lane_r < hd) == (lane_c < hd)
    seg = jnp.where(same_head, 1.0, 0.0)
    ones_cl = jnp.ones((c, LANES), F32)
    ti = lax.broadcasted_iota(jnp.int32, (c, c), 0)
    si = lax.broadcasted_iota(jnp.int32, (c, c), 1)
    tri = jnp.where(ti >= si, 1.0, 0.0)
    rs = lax.broadcasted_iota(jnp.int32, (n2, n2), 0)
    cs = lax.broadcasted_iota(jnp.int32, (n2, n2), 1)
    same_blk = (rs < c) == (cs < c)
    rt = jnp.where(rs < c, rs, rs - c)
    ct = jnp.where(cs < c, cs, cs - c)
    low_s = jnp.logical_and(same_blk, rt > ct)
    low_i = jnp.logical_and(same_blk, rt >= ct)
    eye_n = jnp.where(rs == cs, 1.0, 0.0)
    row_id = lax.broadcasted_iota(jnp.int32, (c, 1), 0)

    def stack(x):
        return jnp.concatenate([x, x], axis=0)

    def stack_heads(x):
        return jnp.concatenate([jnp.where(head0, x, 0.0), jnp.where(head0, 0.0, x)], axis=0)

    def halves(x):
        return jnp.where(head0, x[:c], x[c:])

    subs = range(n_sub)
    rows = [slice(i * c, (i + 1) * c) for i in subs]
    each = lambda f, *xs: [f(*a) for a in zip(*xs)]
    r = [r_ref[rw, :] for rw in rows]
    k = [k_ref[rw, :] for rw in rows]
    v = [v_ref[rw, :] for rw in rows]
    lw = [-jnp.exp(-_softplus(-(w0 + wl_ref[rw, :])) - 0.5) for rw in rows]
    a = [_sigmoid(a0 + al_ref[rw, :]) for rw in rows]
    kk = each(lambda k_: k_ * k_k, k)
    nrm = each(lambda x: jnp.sqrt(_dot_exact_rhs(x * x, seg)), kk)
    kk = each(lambda x, n: x / jnp.maximum(n, 1e-12), kk, nrm)
    kmod = each(lambda k_, a_: k_ * (1.0 + (a_ - 1.0) * k_a), k, a)
    al = each(lambda x: -x, kk)
    be = each(lambda x, a_: x * a_, kk, a)
    kmod_s = kmod
    if t_valid < c * n_sub:
        valid = [(bi * (c * n_sub) + i * c + row_id) < t_valid for i in subs]
        mask = lambda x, m: jnp.where(m, x, 0.0)
        lw, al, be, kmod_s = each(mask, lw, valid), each(mask, al, valid), each(mask, be, valid), each(mask, kmod, valid)

    cum = each(lambda x: _dot_exact_lhs(tri, x), lw)
    p_col = each(lambda x: jnp.exp(_dot_exact_rhs(x, ones_cl, _TN)), lw)
    p_i = each(lambda x: jnp.exp(-x), cum)
    p_e = each(lambda x: jnp.exp(x[c - 1:c, :] - x), cum)
    a_t = each(lambda al_, cu, lw_: al_ * jnp.exp(cu - lw_), al, cum, lw)
    b_t = each(lambda x, p: x * p, be, p_i)
    k_t = each(lambda x, p: x * p, kmod_s, p_i)
    r_t = each(lambda x, cu: x * jnp.exp(cu), r, cum)
    b_bar = each(lambda x, p: x * p, be, p_e)
    k_bar = each(lambda x, p: x * p, kmod_s, p_e)

    x_a = each(stack_heads, a_t)
    x_r = each(stack_heads, r_t)
    y_b = each(stack, b_t)
    y_k = each(stack, k_t)
    v_st = each(stack, v)
    mm = _dot3 if precise else _dot1
    l_ab = each(lambda x, y: jnp.where(low_s, mm(x, y, _NT), 0.0), x_a, y_b)
    l_ak = each(lambda x, y: jnp.where(low_s, mm(x, y, _NT), 0.0), x_a, y_k)
    l_rb = each(lambda x, y: jnp.where(low_i, mm(x, y, _NT), 0.0), x_r, y_b)
    l_rk = each(lambda x, y: jnp.where(low_i, mm(x, y, _NT), 0.0), x_r, y_k)
    t_inv = each(lambda l: eye_n + l, l_ab)
    pw = l_ab
    span = 2
    while span < c:
        pw = each(lambda p: mm(p, p), pw)
        t_inv = each(lambda t, p: t + mm(t, p), t_inv, pw)
        span *= 2
    resid = each(lambda t, l: (eye_n - t) + _dot3(l, t), t_inv, l_ab)
    t_inv = each(lambda t, e: t + mm(t, e), t_inv, resid)
    a_hat = each(lambda t, x: halves(mm(t, stack(x))), t_inv, a_t)
    lv = each(mm, l_ak, v_st)
    u_hat = each(lambda t, x: halves(mm(t, x)), t_inv, lv)
    r_hat = each(lambda x, l, ah: x + halves(mm(l, stack(ah))), r_t, l_rb, a_hat)
    y_hat = each(lambda lb, uh, lk, vs: halves(mm(lb, stack(uh)) + mm(lk, vs)), l_rb, u_hat, l_rk, v_st)
    g_low = each(lambda b, ah: jnp.where(same_head, mm(b, ah, _TN), 0.0), b_bar, a_hat)
    z_mat = each(lambda b, uh, kb, v_: jnp.where(same_head, mm(b, uh, _TN) + mm(kb, v_, _TN), 0.0),
                 b_bar, u_hat, k_bar, v)

    s = s_scr[...]
    y = []
    for i in subs:
        y.append(mm(r_hat[i], s) + y_hat[i])
        s = s * p_col[i] + mm(g_low[i], s) + z_mat[i]

    inv_hd = 1.0 / hd
    stat = _dot_exact_rhs if precise else _dot_sum16
    mean = each(lambda x: stat(x, seg) * inv_hd, y)
    yc = each(lambda x, m: x - m, y, mean)
    var = each(lambda x: stat(x * x, seg) * inv_hd, yc)
    bonus = each(lambda r_, km, v_: stat(r_ * km * r_k, seg) * v_, r, kmod, v)
    for i in subs:
        yn = yc[i] * lax.rsqrt(var[i] + RW_GN_EPS) * ln_w + ln_b
        y_ref[rows[i], :] = ((yn + bonus[i]) * g_ref[rows[i], :]).astype(y_ref.dtype)

    s_scr[...] = s

    @pl.when(bi == pl.num_programs(2) - 1)
    def _():
        sout_ref[...] = s


def _rw_recur(r, k, v, wl, al, g, par, s0, *, n_seq, tg, t_valid):
    rows, d = r.shape
    c = min(64, tg)
    n_sub = min(8, tg // c)
    tb = c * n_sub
    n_blocks = tg // tb
    n_pairs = d // LANES
    tile = pl.BlockSpec((tb, LANES), lambda b, p, bi: (b * n_blocks + bi, p))
    state = pl.BlockSpec((None, None, LANES, LANES), lambda b, p, bi: (b, p, 0, 0))
    return pl.pallas_call(
        functools.partial(_rw_recur_kernel, chunk=c, n_sub=n_sub, t_valid=t_valid, precise=tg <= SAMPLE_SLOT),
        out_shape=[jax.ShapeDtypeStruct((rows, d), BF16),
                   jax.ShapeDtypeStruct((n_seq, n_pairs, LANES, LANES), F32)],
        grid=(n_seq, n_pairs, n_blocks),
        in_specs=[tile] * 6 + [pl.BlockSpec((8, LANES), lambda b, p, bi: (0, p)), state],
        out_specs=[tile, state],
        scratch_shapes=[pltpu.VMEM((LANES, LANES), F32)],
        compiler_params=_params("parallel", "parallel", "arbitrary"),
        name="rw_recur",
    )(r, k, v, wl, al, g, par, s0)


def _rw_state_to_pairs(wkv):
    b, h, n, _ = wkv.shape
    st = jnp.swapaxes(wkv, -1, -2).reshape(b, h // 2, 2, n, n)
    z = jnp.zeros_like(st[:, :, 0])
    top = jnp.concatenate([st[:, :, 0], z], axis=-1)
    bot = jnp.concatenate([z, st[:, :, 1]], axis=-1)
    return jnp.concatenate([top, bot], axis=-2)


def _rw_pairs_to_state(s):
    b, p, _, _ = s.shape
    n = RW_HEAD_DIM
    h0 = s[:, :, :n, :n]
    h1 = s[:, :, n:, n:]
    st = jnp.stack([h0, h1], axis=2).reshape(b, 2 * p, n, n)
    return jnp.swapaxes(st, -1, -2)


def _rwkv7(h, shift_prev, wkv_prev, w, *, n_seq, tg, t_valid):
    rows, d = h.shape
    h3 = h.reshape(n_seq, tg, d)
    h_prev = jnp.concatenate([shift_prev[:, None, :], h3[:, :-1]], axis=1).reshape(rows, d)
    xr, xw, xk, xv, xa, xg = _rw_mix(h, h_prev, w["rw_mu"])
    r = _matmul(xr, w["rw_w_r"])
    k = _matmul(xk, w["rw_w_k"])
    v = _matmul(xv, w["rw_w_v"])
    wl = _matmul(_matmul(xw, w["rw_w1"], out_dtype=BF16, act="tanh"), w["rw_w2"])
    al = _matmul(_matmul(xa, w["rw_a1"], out_dtype=BF16), w["rw_a2"])
    g = _matmul(_matmul(xg, w["rw_g1"], out_dtype=BF16, act="sigmoid"), w["rw_g2"])
    par = jnp.stack([w["rw_w0"], w["rw_a0"], w["rw_k_k"], w["rw_k_a"], w["rw_r_k"].reshape(-1),
                     w["rw_ln_w"], w["rw_ln_b"], jnp.zeros((d,), F32)])
    y, s_fin = _rw_recur(r, k, v, wl, al, g, par, _rw_state_to_pairs(wkv_prev),
                         n_seq=n_seq, tg=tg, t_valid=t_valid)
    return y, h3[:, t_valid - 1], _rw_pairs_to_state(s_fin)


def _to_heads_kernel(x_ref, o_ref, *, n_heads, hd):
    tm = x_ref.shape[0]
    for h in range(n_heads):
        o_ref[pl.ds(h, tm, stride=n_heads), :] = x_ref[:, h * hd:(h + 1) * hd]


def _to_heads(x, third, n_seq, tg, head_shape):
    rows = x.shape[0]
    hd = head_shape[-1]
    n_heads = math.prod(head_shape[:-1])
    d = n_heads * hd
    tm = min(256, rows)
    out = pl.pallas_call(
        functools.partial(_to_heads_kernel, n_heads=n_heads, hd=hd),
        out_shape=jax.ShapeDtypeStruct((rows * n_heads, hd), x.dtype),
        grid=(rows // tm,),
        in_specs=[pl.BlockSpec((tm, d), lambda i: (i, third))],
        out_specs=pl.BlockSpec((tm * n_heads, hd), lambda i: (i, 0)),
        compiler_params=_params("parallel"),
        name="to_heads",
    )(x)
    return out.reshape((n_seq, tg) + tuple(head_shape))


def _sb_tile(z, mask, later_carry, u_mat):
    lk0 = -_softplus(z)
    lk = lk0 if mask is None else jnp.where(mask, lk0, 0.0)
    later = _dot_sum16(lk, u_mat)
    w = jnp.exp(z + lk0 + later + later_carry)
    if mask is not None:
        w = jnp.where(mask, w, 0.0)
    return w, later_carry + later[:, 0:1] + lk[:, 0:1]


def _suffix_matrix(n):
    j = lax.broadcasted_iota(jnp.int32, (n, n), 0)
    k = lax.broadcasted_iota(jnp.int32, (n, n), 1)
    return jnp.where(j > k, 1.0, 0.0)


def _sb_prompt_kernel(q_ref, k_ref, v_ref, o_ref, *, bq, n_par):
    qi = pl.program_id(2)
    hd = SB_HEAD_DIM
    scale = hd ** -0.5
    heads = range(n_par)
    cols = [slice(h * hd, (h + 1) * hd) for h in heads]
    q = [q_ref[:, c].astype(BF16) for c in cols]
    u_mat = _suffix_matrix(bq)
    row = lax.broadcasted_iota(jnp.int32, (bq, bq), 0)
    col = lax.broadcasted_iota(jnp.int32, (bq, bq), 1)

    def block(kb, state, mask):
        start = pl.multiple_of(kb * bq, bq)
        z = [lax.dot_general(q[h], k_ref[pl.ds(start, bq), cols[h]].astype(BF16), _NT,
                             preferred_element_type=F32) * scale for h in heads]
        tiles = [_sb_tile(z[h], mask, state[h][1], u_mat) for h in heads]
        return tuple(
            (state[h][0] + jnp.dot(tiles[h][0].astype(BF16), v_ref[pl.ds(start, bq), cols[h]].astype(BF16),
                                   preferred_element_type=F32), tiles[h][1]) for h in heads)

    init = tuple((jnp.zeros((bq, hd), F32), jnp.zeros((bq, 1), F32)) for _ in heads)
    state = block(qi, init, col < row)
    state = lax.fori_loop(1, qi + 1, lambda j, st: block(qi - j, st, None), state)
    for h in heads:
        o_ref[:, cols[h]] = state[h][0].astype(o_ref.dtype)


def _sb_prompt(qkv, *, n_seq, tg):
    rows, d3 = qkv.shape
    d = d3 // 3
    hd = SB_HEAD_DIM
    n_heads = d // hd
    n_par = 2 if n_heads % 2 == 0 else 1
    hw = n_par * hd
    n_hb = n_heads // n_par
    bq = min(256, tg)
    nq = tg // bq
    return pl.pallas_call(
        functools.partial(_sb_prompt_kernel, bq=bq, n_par=n_par),
        out_shape=jax.ShapeDtypeStruct((rows, d), BF16),
        grid=(n_seq, n_hb, nq),
        in_specs=[
            pl.BlockSpec((bq, hw), lambda b, h, qi: (b * nq + qi, h)),
            pl.BlockSpec((tg, hw), lambda b, h, qi: (b, n_hb + h)),
            pl.BlockSpec((tg, hw), lambda b, h, qi: (b, 2 * n_hb + h)),
        ],
        out_specs=pl.BlockSpec((bq, hw), lambda b, h, qi: (b * nq + qi, h)),
        compiler_params=_params("parallel", "parallel", "arbitrary"),
        name="sb_prompt",
    )(qkv, qkv, qkv)


def _sb_sample_kernel(pt_ref, q_ref, kn_ref, vn_ref, kc_ref, vc_ref, o_ref, acc_scr, carry_scr,
                      *, n_heads, t_valid, page):
    j = pl.program_id(1)
    hd = SB_HEAD_DIM
    slot = SAMPLE_SLOT
    scale = hd ** -0.5
    n_rows = n_heads * slot

    def tile(get_k, get_v, nk, first):
        q = q_ref[...].astype(BF16)
        z = jnp.concatenate(
            [lax.dot_general(q[:, h * hd:(h + 1) * hd], get_k(h).astype(BF16), _NT, preferred_element_type=F32)
             for h in range(n_heads)], axis=0) * scale
        if first:
            row = lax.broadcasted_iota(jnp.int32, (n_rows, nk), 0)
            col = lax.broadcasted_iota(jnp.int32, (n_rows, nk), 1)
            mask = col < jnp.minimum(row & (slot - 1), t_valid)
            carry = jnp.zeros((n_rows, 1), F32)
        else:
            mask = None
            carry = carry_scr[:, 0:1]
        w, carry = _sb_tile(z, mask, carry, _suffix_matrix(nk))
        carry_scr[...] = jnp.broadcast_to(carry, (n_rows, LANES))
        wb = w.astype(BF16)
        for h in range(n_heads):
            pv = jnp.dot(wb[h * slot:(h + 1) * slot], get_v(h).astype(BF16), preferred_element_type=F32)
            sl = slice(h * hd, (h + 1) * hd)
            acc_scr[:, sl] = pv if first else acc_scr[:, sl] + pv

    @pl.when(j == 0)
    def _():
        tile(lambda h: kn_ref[:, h * hd:(h + 1) * hd], lambda h: vn_ref[:, h * hd:(h + 1) * hd], slot, True)

    @pl.when(j > 0)
    def _():
        tile(lambda h: kc_ref[pl.ds(h, page, stride=n_heads), :],
             lambda h: vc_ref[pl.ds(h, page, stride=n_heads), :], page, False)

    @pl.when(j == pl.num_programs(1) - 1)
    def _():
        o_ref[...] = acc_scr[...].astype(o_ref.dtype)


def _sb_sample(qkv, cache_k, cache_v, page_table, *, t_valid):
    n_seq, n_pages = page_table.shape
    rows, d3 = qkv.shape
    d = d3 // 3
    n_phys, page, n_heads, hd = cache_k.shape
    slot = SAMPLE_SLOT

    ck = cache_k.reshape(n_phys, page * n_heads, hd)
    cv = cache_v.reshape(n_phys, page * n_heads, hd)
    page_spec = pl.BlockSpec((None, page * n_heads, hd), lambda s, j, pt: (pt[s, n_pages - jnp.maximum(j, 1)], 0, 0))

    grid_spec = pltpu.PrefetchScalarGridSpec(
        num_scalar_prefetch=1,
        grid=(n_seq, n_pages + 1),
        in_specs=[
            pl.BlockSpec((slot, d), lambda s, j, pt: (s, 0)),
            pl.BlockSpec((slot, d), lambda s, j, pt: (s, 1)),
            pl.BlockSpec((slot, d), lambda s, j, pt: (s, 2)),
            page_spec,
            page_spec,
        ],
        out_specs=pl.BlockSpec((slot, d), lambda s, j, pt: (s, 0)),
        scratch_shapes=[pltpu.VMEM((slot, d), F32), pltpu.VMEM((n_heads * slot, LANES), F32)],
    )
    return pl.pallas_call(
        functools.partial(_sb_sample_kernel, n_heads=n_heads, t_valid=t_valid, page=page),
        out_shape=jax.ShapeDtypeStruct((rows, d), BF16),
        grid_spec=grid_spec,
        compiler_params=_params("parallel", "arbitrary"),
        name="sb_sample",
    )(page_table, qkv, qkv, qkv, ck, cv)


def _rope_kernel(x_ref, cs_ref, sn_ref, o_ref):
    cs = cs_ref[...]
    sn = sn_ref[...]
    for c in range(x_ref.shape[1] // LANES):
        x = x_ref[:, c * LANES:(c + 1) * LANES]
        o_ref[:, c * LANES:(c + 1) * LANES] = x * cs + pltpu.roll(x, LANES // 2, 1) * sn


def _rope_tables(pos):
    half = DA_HEAD_DIM // 2
    inv_freq = ROPE_THETA ** (-jnp.arange(half, dtype=F32) / half)
    ang = pos.astype(F32)[:, None] * inv_freq[None, :]
    cos, sin = jnp.cos(ang), jnp.sin(ang)
    return jnp.concatenate([cos, cos], axis=1), jnp.concatenate([-sin, sin], axis=1)


def _rope_qk(qkv, pos_rows):
    rows, d3 = qkv.shape
    d2 = d3 // 3 * 2
    cs, sn = _rope_tables(pos_rows)
    tt = cs.shape[0]
    tm = min(256, tt)
    tn = min(1024, d2)
    per = tt // tm
    return pl.pallas_call(
        _rope_kernel,
        out_shape=jax.ShapeDtypeStruct((rows, d2), F32),
        grid=(rows // tm, d2 // tn),
        in_specs=[
            pl.BlockSpec((tm, tn), lambda i, j: (i, j)),
            pl.BlockSpec((tm, LANES), lambda i, j: (i % per, 0)),
            pl.BlockSpec((tm, LANES), lambda i, j: (i % per, 0)),
        ],
        out_specs=pl.BlockSpec((tm, tn), lambda i, j: (i, j)),
        compiler_params=_params("parallel", "parallel"),
        name="rope",
    )(qkv, cs, sn)


def _softmax_tile(s, mask, m_prev, l_prev):
    if mask is not None:
        s = jnp.where(mask, s, -jnp.inf)
    m_new = jnp.maximum(m_prev, jnp.max(s, axis=-1, keepdims=True))
    p = jnp.exp(s - m_new)
    alpha = jnp.exp(m_prev - m_new)
    return p, m_new, alpha, alpha * l_prev + jnp.sum(p, axis=-1, keepdims=True)


def _da_finish(acc1, l1, acc2, l2, lam, subln, lam_init):
    o = acc1 / l1 - lam * (acc2 / l2)
    o = o * lax.rsqrt(jnp.mean(o * o, axis=-1, keepdims=True) + DA_SUBLN_EPS) * subln
    return o * (1.0 - lam_init)


def _da_prompt_kernel(lam_ref, q_ref, k_ref, v_ref, sub_ref, o_ref, *, bq, lam_init):
    qi = pl.program_id(2)
    hd = DA_HEAD_DIM
    scale = hd ** -0.5
    q1 = q_ref[:, :hd].astype(BF16)
    q2 = q_ref[:, hd:].astype(BF16)
    row = lax.broadcasted_iota(jnp.int32, (bq, bq), 0)
    col = lax.broadcasted_iota(jnp.int32, (bq, bq), 1)

    def block(kb, state, mask):
        m1, l1, acc1, m2, l2, acc2 = state
        start = pl.multiple_of(kb * bq, bq)
        k_blk = k_ref[pl.ds(start, bq), :].astype(BF16)
        v_blk = v_ref[pl.ds(start, bq), :].astype(BF16)
        s1 = lax.dot_general(q1, k_blk[:, :hd], _NT, preferred_element_type=F32) * scale
        s2 = lax.dot_general(q2, k_blk[:, hd:], _NT, preferred_element_type=F32) * scale
        p1, m1, a1, l1 = _softmax_tile(s1, mask, m1, l1)
        p2, m2, a2, l2 = _softmax_tile(s2, mask, m2, l2)
        acc1 = a1 * acc1 + jnp.dot(p1.astype(BF16), v_blk, preferred_element_type=F32)
        acc2 = a2 * acc2 + jnp.dot(p2.astype(BF16), v_blk, preferred_element_type=F32)
        return m1, l1, acc1, m2, l2, acc2

    neg = jnp.full((bq, 1), -jnp.inf, F32)
    zero = jnp.zeros((bq, 1), F32)
    zacc = jnp.zeros((bq, 2 * hd), F32)
    state = lax.fori_loop(0, qi, lambda kb, st: block(kb, st, None), (neg, zero, zacc, neg, zero, zacc))
    _, l1, acc1, _, l2, acc2 = block(qi, state, col <= row)
    o_ref[...] = _da_finish(acc1, l1, acc2, l2, lam_ref[0], sub_ref[...], lam_init).astype(o_ref.dtype)


def _da_prompt(qk, qkv, lam, subln, *, n_seq, tg, lam_init):
    rows, d2 = qk.shape
    d = d2 // 2
    hw = 2 * DA_HEAD_DIM
    n_heads = d // hw
    bq = min(256, tg)
    nq = tg // bq
    return pl.pallas_call(
        functools.partial(_da_prompt_kernel, bq=bq, lam_init=lam_init),
        out_shape=jax.ShapeDtypeStruct((rows, d), BF16),
        grid=(n_seq, n_heads, nq),
        in_specs=[
            pl.BlockSpec(memory_space=pltpu.SMEM),
            pl.BlockSpec((bq, hw), lambda b, h, qi: (b * nq + qi, h)),
            pl.BlockSpec((tg, hw), lambda b, h, qi: (b, n_heads + h)),
            pl.BlockSpec((tg, hw), lambda b, h, qi: (b, 2 * n_heads + h)),
            pl.BlockSpec((1, hw), lambda b, h, qi: (0, 0)),
        ],
        out_specs=pl.BlockSpec((bq, hw), lambda b, h, qi: (b * nq + qi, h)),
        compiler_params=_params("parallel", "parallel", "arbitrary"),
        name="da_prompt",
    )(lam, qk, qk, qkv, subln.reshape(1, hw))


def _da_sample_kernel(pt_ref, q_ref, kn_ref, vn_ref, kc_ref, vlo_ref, vhi_ref, lam_ref, sub_ref, o_ref,
                      m_scr, l_scr, acc_scr, *, n_heads, t_valid, page, lam_init):
    j = pl.program_id(1)
    hd = DA_HEAD_DIM
    hw = 2 * hd
    slot = SAMPLE_SLOT
    scale = hd ** -0.5
    n_rows = 2 * n_heads * slot

    def tile(get_k, get_v, nk, first):
        q = q_ref[...].astype(BF16)
        s = jnp.concatenate(
            [lax.dot_general(q[:, i * hd:(i + 1) * hd], get_k(i // 2, i % 2).astype(BF16), _NT,
                             preferred_element_type=F32) for i in range(2 * n_heads)], axis=0) * scale
        if first:
            row = lax.broadcasted_iota(jnp.int32, (n_rows, nk), 0)
            col = lax.broadcasted_iota(jnp.int32, (n_rows, nk), 1)
            mask = col <= jnp.minimum(row & (slot - 1), t_valid - 1)
            m_prev = jnp.full((n_rows, 1), -jnp.inf, F32)
            l_prev = jnp.zeros((n_rows, 1), F32)
        else:
            mask = None
            m_prev = m_scr[:, 0:1]
            l_prev = l_scr[:, 0:1]
        p, m_new, alpha, l_new = _softmax_tile(s, mask, m_prev, l_prev)
        m_scr[...] = jnp.broadcast_to(m_new, (n_rows, LANES))
        l_scr[...] = jnp.broadcast_to(l_new, (n_rows, LANES))
        pb = p.astype(BF16)
        for h in range(n_heads):
            vb = get_v(h).astype(BF16)
            sl = slice(h * hw, (h + 1) * hw)
            for c in range(2):
                rows = slice((2 * h + c) * slot, (2 * h + c + 1) * slot)
                pv = jnp.dot(pb[rows], vb, preferred_element_type=F32)
                acc_scr[c, :, sl] = pv if first else alpha[rows] * acc_scr[c, :, sl] + pv

    @pl.when(j == 0)
    def _():
        tile(lambda h, c: kn_ref[:, (2 * h + c) * hd:(2 * h + c + 1) * hd],
             lambda h: vn_ref[:, h * hw:(h + 1) * hw], slot, True)

    @pl.when(j > 0)
    def _():
        tile(lambda h, c: kc_ref[pl.ds(2 * h + c, page, stride=2 * n_heads), :],
             lambda h: jnp.concatenate([vlo_ref[pl.ds(h, page, stride=n_heads), :],
                                        vhi_ref[pl.ds(h, page, stride=n_heads), :]], axis=1), page, False)

    @pl.when(j == pl.num_programs(1) - 1)
    def _():
        l_all = l_scr[:, 0:1]
        for h in range(n_heads):
            sl = slice(h * hw, (h + 1) * hw)
            r1 = slice(2 * h * slot, (2 * h + 1) * slot)
            r2 = slice((2 * h + 1) * slot, (2 * h + 2) * slot)
            o = _da_finish(acc_scr[0, :, sl], l_all[r1], acc_scr[1, :, sl], l_all[r2],
                           lam_ref[0], sub_ref[...], lam_init)
            o_ref[:, sl] = o.astype(o_ref.dtype)


def _da_sample(qk, qkv, cache_k, cache_v, page_table, lam, subln, *, t_valid, lam_init):
    rows, d2 = qk.shape
    d = d2 // 2
    hd = DA_HEAD_DIM
    hw = 2 * hd
    n_heads = d // hw
    n_seq, n_pages = page_table.shape
    n_phys, page = cache_k.shape[:2]
    ck = cache_k.reshape(n_phys, page * 2 * n_heads, hd)
    cv = cache_v.reshape(n_phys, page * n_heads, hw)
    slot = SAMPLE_SLOT

    def page_map(half):
        return lambda s, j, pt: (pt[s, n_pages - jnp.maximum(j, 1)], 0, half)

    grid_spec = pltpu.PrefetchScalarGridSpec(
        num_scalar_prefetch=1,
        grid=(n_seq, n_pages + 1),
        in_specs=[
            pl.BlockSpec((slot, d), lambda s, j, pt: (s, 0)),
            pl.BlockSpec((slot, d), lambda s, j, pt: (s, 1)),
            pl.BlockSpec((slot, d), lambda s, j, pt: (s, 2)),
            pl.BlockSpec((None, page * 2 * n_heads, hd), page_map(0)),
            pl.BlockSpec((None, page * n_heads, hd), page_map(0)),
            pl.BlockSpec((None, page * n_heads, hd), page_map(1)),
            pl.BlockSpec(memory_space=pltpu.SMEM),
            pl.BlockSpec((1, hw), lambda s, j, pt: (0, 0)),
        ],
        out_specs=pl.BlockSpec((slot, d), lambda s, j, pt: (s, 0)),
        scratch_shapes=[pltpu.VMEM((2 * n_heads * slot, LANES), F32),
                        pltpu.VMEM((2 * n_heads * slot, LANES), F32),
                        pltpu.VMEM((2, slot, d), F32)],
    )
    return pl.pallas_call(
        functools.partial(_da_sample_kernel, n_heads=n_heads, t_valid=t_valid, page=page, lam_init=lam_init),
        out_shape=jax.ShapeDtypeStruct((rows, d), BF16),
        grid_spec=grid_spec,
        compiler_params=_params("parallel", "arbitrary"),
        name="da_sample",
    )(page_table, qk, qk, qkv, ck, cv, cv, lam, subln.reshape(1, hw))


def _gla_kernel(q_ref, k_ref, v_ref, gate_ref, lo_ref, ba_ref, nw_ref, s0_ref, o_ref, sout_ref, s_scr,
                *, tb, c, tg, t_valid, dk):
    ci = pl.program_id(2)

    @pl.when(ci == 0)
    def _():
        s_scr[...] = s0_ref[...]

    ti = lax.broadcasted_iota(jnp.int32, (c, c), 0)
    si = lax.broadcasted_iota(jnp.int32, (c, c), 1)
    tri = jnp.where(ti >= si, 1.0, 0.0)
    row_id = lax.broadcasted_iota(jnp.int32, (c, 1), 0)
    ones_cl = jnp.ones((c, LANES), F32)
    ba = ba_ref[...]
    nw = nw_ref[...]
    for sub in range(tb // c):
        rows = slice(sub * c, (sub + 1) * c)
        q = q_ref[rows, :] * (dk ** -0.5)
        k = k_ref[rows, :]
        v = v_ref[rows, :]
        la = -_softplus(-(lo_ref[rows, :] + ba)) / GLA_TAU
        if t_valid < tg:
            valid = (ci * tb + sub * c + row_id) < t_valid
            la = jnp.where(valid, la, 0.0)
            k = jnp.where(valid, k, 0.0)
        cum = _dot_exact_lhs(tri, la)
        cum_end = cum[c - 1:c, :]
        s = s_scr[...]
        o = _dot1(q * jnp.exp(cum), s)
        att = jnp.zeros((c, c), F32)
        for j in range(c):
            e = jnp.exp(jnp.where(row_id >= j, cum - cum[j:j + 1, :], -jnp.inf))
            p = jnp.sum(q * k[j:j + 1, :] * e, axis=-1, keepdims=True)
            att = jnp.where(si == j, p, att)
        o = o + _dot1(att, v)
        dec = jnp.exp(_dot_exact_rhs(la, ones_cl, _TN)[:, 0:1])
        s_scr[...] = s * dec + _dot1(k * jnp.exp(cum_end - cum), v, _TN)
        on = o * lax.rsqrt(jnp.mean(o * o, axis=-1, keepdims=True) + GLA_NORM_EPS) * nw
        g = gate_ref[rows, :]
        o_ref[rows, :] = (on * (g * _sigmoid(g))).astype(o_ref.dtype)

    @pl.when(ci == pl.num_programs(2) - 1)
    def _():
        sout_ref[...] = s_scr[...]


def _gla(proj, lo, ba, norm_w, s0, *, n_seq, tg, t_valid):
    rows = proj.shape[0]
    key = lo.shape[1]
    d = (proj.shape[1] - 2 * key) // 2
    nh = GLA_HEADS
    dk, dv = key // nh, d // nh
    tb = min(128, tg)
    c = min(32, tb)
    nc = tg // tb
    kv = key // dv
    return pl.pallas_call(
        functools.partial(_gla_kernel, tb=tb, c=c, tg=tg, t_valid=t_valid, dk=dk),
        out_shape=[jax.ShapeDtypeStruct((rows, d), BF16), jax.ShapeDtypeStruct((n_seq, nh, dk, dv), F32)],
        grid=(n_seq, nh, nc),
        in_specs=[
            pl.BlockSpec((tb, dk), lambda b, h, ci: (b * nc + ci, h)),
            pl.BlockSpec((tb, dk), lambda b, h, ci: (b * nc + ci, nh + h)),
            pl.BlockSpec((tb, dv), lambda b, h, ci: (b * nc + ci, 2 * kv + h)),
            pl.BlockSpec((tb, dv), lambda b, h, ci: (b * nc + ci, 2 * kv + nh + h)),
            pl.BlockSpec((tb, dk), lambda b, h, ci: (b * nc + ci, h)),
            pl.BlockSpec((1, dk), lambda b, h, ci: (0, h)),
            pl.BlockSpec((1, dv), lambda b, h, ci: (0, 0)),
            pl.BlockSpec((None, None, dk, dv), lambda b, h, ci: (b, h, 0, 0)),
        ],
        out_specs=[pl.BlockSpec((tb, dv), lambda b, h, ci: (b * nc + ci, h)),
                   pl.BlockSpec((None, None, dk, dv), lambda b, h, ci: (b, h, 0, 0))],
        scratch_shapes=[pltpu.VMEM((dk, dv), F32)],
        compiler_params=_params("parallel", "parallel", "arbitrary"),
        name="gla",
    )(proj, proj, proj, proj, lo, ba.reshape(1, key), norm_w.reshape(1, dv), s0)


MOE_BLOCK_ROWS = 256


def _moe_route(logits, n_groups, n_experts):
    g_logits = logits[:, :n_groups]
    grp = jnp.argmax(g_logits, axis=-1)
    g_w = jnp.take_along_axis(jax.nn.softmax(g_logits, axis=-1), grp[:, None], axis=-1)
    e_logits = logits[:, n_groups:n_groups + n_groups * n_experts].reshape(-1, n_groups, n_experts)
    e_logits = jnp.take_along_axis(e_logits, grp[:, None, None], axis=1)[:, 0]
    top_val, top_idx = lax.top_k(e_logits, MOE_TOPK)
    gate = g_w * jax.nn.softmax(top_val, axis=-1)
    return (grp[:, None] * n_experts + top_idx).astype(jnp.int32), gate


def _moe_plan(expert_id, n_exp):
    tm = MOE_BLOCK_ROWS
    n_tok, top_k = expert_id.shape
    n_assign = n_tok * top_k
    e_flat = expert_id.reshape(-1)
    order = jnp.argsort(e_flat)
    inv = jnp.argsort(order)
    counts = jnp.bincount(e_flat, length=n_exp)
    start = jnp.cumsum(counts) - counts
    padded = (counts + tm - 1) // tm * tm
    pad_end = jnp.cumsum(padded)
    pad_start = pad_end - padded
    pos = (inv + (pad_start - start)[e_flat]).astype(jnp.int32).reshape(n_tok, top_k)
    n_blocks = -(-n_assign // tm) + n_exp
    block_expert = jnp.minimum(jnp.searchsorted(pad_end, jnp.arange(n_blocks) * tm, side="right"), n_exp - 1)
    row = jnp.arange(n_blocks * tm)
    e_row = jnp.repeat(block_expert, tm)
    rank = row - pad_start[e_row]
    src = order[jnp.clip(rank + start[e_row], 0, n_assign - 1)] // top_k
    src_tok = jnp.where(rank < counts[e_row], src, 0).astype(jnp.int32)
    n_used = (pad_end[-1] // tm).astype(jnp.int32).reshape(1)
    return src_tok, pos, block_expert.astype(jnp.int32), n_used


def _new_expert(be_ref, b):
    return jnp.logical_or(b == 0, be_ref[b] != be_ref[jnp.maximum(b - 1, 0)])


def _moe_up_kernel(be_ref, nu_ref, x_ref, wg_ref, wu_ref, h_ref, wg_bf, wu_bf):
    b = pl.program_id(0)
    used = b < nu_ref[0]

    @pl.when(jnp.logical_and(used, _new_expert(be_ref, b)))
    def _():
        wg_bf[...] = wg_ref[...].astype(BF16)
        wu_bf[...] = wu_ref[...].astype(BF16)

    @pl.when(used)
    def _():
        x = x_ref[...]
        g = jnp.dot(x, wg_bf[...], preferred_element_type=F32)
        u = jnp.dot(x, wu_bf[...], preferred_element_type=F32)
        h_ref[...] = (g * _sigmoid(g) * u).astype(h_ref.dtype)

    @pl.when(jnp.logical_not(used))
    def _():
        h_ref[...] = jnp.zeros_like(h_ref)


def _moe_down_kernel(be_ref, nu_ref, h_ref, wd_ref, o_ref, wd_bf):
    b = pl.program_id(0)
    used = b < nu_ref[0]

    @pl.when(jnp.logical_and(used, _new_expert(be_ref, b)))
    def _():
        wd_bf[...] = wd_ref[...].astype(BF16)

    @pl.when(used)
    def _():
        o_ref[...] = jnp.dot(h_ref[...], wd_bf[...], preferred_element_type=F32)

    @pl.when(jnp.logical_not(used))
    def _():
        o_ref[...] = jnp.zeros_like(o_ref)


def _moe_experts(rows, block_expert, n_used, w_gate, w_up, w_down, layer):
    tm = MOE_BLOCK_ROWS
    n_rows, d = rows.shape
    f = w_gate.shape[3]
    n_blocks = n_rows // tm
    up_spec = pltpu.PrefetchScalarGridSpec(
        num_scalar_prefetch=2,
        grid=(n_blocks,),
        in_specs=[
            pl.BlockSpec((tm, d), lambda b, be, nu: (b, 0)),
            pl.BlockSpec((None, None, d, f), lambda b, be, nu: (layer, be[b], 0, 0)),
            pl.BlockSpec((None, None, d, f), lambda b, be, nu: (layer, be[b], 0, 0)),
        ],
        out_specs=pl.BlockSpec((tm, f), lambda b, be, nu: (b, 0)),
        scratch_shapes=[pltpu.VMEM((d, f), BF16), pltpu.VMEM((d, f), BF16)],
    )
    h = pl.pallas_call(
        _moe_up_kernel,
        out_shape=jax.ShapeDtypeStruct((n_rows, f), BF16),
        grid_spec=up_spec,
        compiler_params=_params("arbitrary"),
        name="moe_up",
    )(block_expert, n_used, rows, w_gate, w_up)
    down_spec = pltpu.PrefetchScalarGridSpec(
        num_scalar_prefetch=2,
        grid=(n_blocks,),
        in_specs=[
            pl.BlockSpec((tm, f), lambda b, be, nu: (b, 0)),
            pl.BlockSpec((None, None, f, d), lambda b, be, nu: (layer, be[b], 0, 0)),
        ],
        out_specs=pl.BlockSpec((tm, d), lambda b, be, nu: (b, 0)),
        scratch_shapes=[pltpu.VMEM((f, d), BF16)],
    )
    return pl.pallas_call(
        _moe_down_kernel,
        out_shape=jax.ShapeDtypeStruct((n_rows, d), F32),
        grid_spec=down_spec,
        compiler_params=_params("arbitrary"),
        name="moe_down",
    )(block_expert, n_used, h, w_down)


def _moe_combine_kernel(x_ref, o0_ref, o1_ref, g_ref, m_ref, y_ref):
    g = g_ref[...]
    y = o0_ref[...] * g[:, 0:1] + o1_ref[...] * g[:, 1:2]
    y_ref[...] = x_ref[...] + m_ref[...] * y


def _moe_combine(x, o0, o1, gates, gmod, *, tg):
    rows, d = x.shape
    tm = min(256, tg) if gmod.shape[1] == 1 else rows
    tile = pl.BlockSpec((tm, d), lambda i: (i, 0))
    return pl.pallas_call(
        _moe_combine_kernel,
        out_shape=jax.ShapeDtypeStruct((rows, d), F32),
        grid=(rows // tm,),
        in_specs=[tile, tile, tile, pl.BlockSpec((tm, MOE_TOPK), lambda i: (i, 0)), _mod_spec(gmod, tm, tg, d)],
        out_specs=tile,
        compiler_params=_params("parallel"),
        name="moe_combine",
    )(x, o0, o1, gates, gmod)


def _moe_layer(groups, xs, hs, logits, gmods, n_groups, w_gate, w_up, w_down, layer):
    n_experts = w_gate.shape[1] // n_groups
    d = xs[0].shape[1]
    real_h, real_l = [], []
    for grp, h, lg in zip(groups, hs, logits):
        n_seq, tg, tv = grp
        real_h.append(h.reshape(n_seq, tg, d)[:, :tv].reshape(n_seq * tv, d))
        real_l.append(lg.reshape(n_seq, tg, -1)[:, :tv].reshape(n_seq * tv, -1))
    h_all = jnp.concatenate(real_h, axis=0)
    expert_id, gate = _moe_route(jnp.concatenate(real_l, axis=0), n_groups, n_experts)
    src_tok, pos, block_expert, n_used = _moe_plan(expert_id, n_groups * n_experts)
    out = _moe_experts(h_all[src_tok], block_expert, n_used, w_gate, w_up, w_down, layer)
    new_xs = []
    start = 0
    for grp, x, gmod in zip(groups, xs, gmods):
        n_seq, tg, tv = grp
        n = n_seq * tv
        rows = pos[start:start + n]
        parts = []
        for a in (out[rows[:, 0]], out[rows[:, 1]], gate[start:start + n]):
            a = a.reshape(n_seq, tv, -1)
            if tv < tg:
                a = jnp.pad(a, ((0, 0), (0, tg - tv), (0, 0)))
            parts.append(a.reshape(n_seq * tg, -1))
        new_xs.append(_moe_combine(x, parts[0], parts[1], parts[2], gmod, tg=tg))
        start += n
    return new_xs


def kernel(x_prompt, x_sample, c_prompt, c_sample, state_rwkv_shift, state_rwkv_wkv, cache_sb_k, cache_sb_v, cache_da_k, cache_da_v, state_gla, page_table, ada_w, ada_b, norm_mix, norm_ffn, norm_final, rw_mu, rw_w_r, rw_w_k, rw_w_v, rw_w_o, rw_w0, rw_w1, rw_w2, rw_a0, rw_a1, rw_a2, rw_g1, rw_g2, rw_k_k, rw_k_a, rw_r_k, rw_ln_w, rw_ln_b, sb_w_in, sb_w_out, da_w_in, da_w_out, da_lambda, da_subln, gla_w_in, gla_wa1, gla_wa2, gla_ba, gla_norm, gla_w_out, moe_w_group, moe_w_expert, moe_w_gate, moe_w_up, moe_w_down):
    n_p, t_p, d = x_prompt.shape
    n_s, t_s, _ = x_sample.shape
    depth = ada_w.shape[0]
    slot = SAMPLE_SLOT
    n_pages, page = page_table.shape[1], cache_sb_k.shape[1]
    past_len = n_pages * page
    groups = [(n_p, t_p, t_p), (n_s, slot, t_s)]

    n_c = n_p + n_s
    c_all = jnp.concatenate([c_prompt, c_sample], axis=0)
    c_all = jnp.pad(c_all, ((0, -n_c % SUBLANES), (0, 0)))
    mod = _adaln(c_all, ada_w, ada_b)

    xs = [x_prompt.reshape(n_p * t_p, d),
          jnp.pad(x_sample, ((0, 0), (0, slot - t_s), (0, 0))).reshape(n_s * slot, d)]
    rw = dict(rw_mu=rw_mu, rw_w0=rw_w0, rw_a0=rw_a0, rw_k_k=rw_k_k, rw_k_a=rw_k_a, rw_r_k=rw_r_k,
              rw_ln_w=rw_ln_w, rw_ln_b=rw_ln_b)
    for name, val in (("rw_w_r", rw_w_r), ("rw_w_k", rw_w_k), ("rw_w_v", rw_w_v), ("rw_w1", rw_w1),
                      ("rw_w2", rw_w2), ("rw_a1", rw_a1), ("rw_a2", rw_a2), ("rw_g1", rw_g1), ("rw_g2", rw_g2)):
        rw[name] = val.astype(BF16)
    pos = [jnp.arange(t_p, dtype=jnp.int32), jnp.tile(past_len + jnp.arange(slot, dtype=jnp.int32), n_s)]
    rw_state = [(jnp.zeros((n_p, d), F32), jnp.zeros((n_p, d // RW_HEAD_DIM, RW_HEAD_DIM, RW_HEAD_DIM), F32)),
                (state_rwkv_shift, state_rwkv_wkv)]
    gla_state = [jnp.zeros((n_p,) + state_gla.shape[1:], F32), state_gla]
    outs = [dict(), dict()]

    for i in range(depth):
        mods = []
        for gi, (n_seq, tg, tv) in enumerate(groups):
            rows = mod[i, :n_p] if gi == 0 else mod[i, n_p:n_c]
            six = [rows[:, j * d:(j + 1) * d] for j in range(6)]
            if gi == 0:
                mods.append([m[:, None, :] for m in six])
            else:
                mods.append([jnp.repeat(m, tg, axis=0)[None] for m in six])
        kind = i % 4
        for gi, (n_seq, tg, tv) in enumerate(groups):
            m = mods[gi]
            x = xs[gi]
            o = outs[gi]

            def rows_of(a, width_shape):
                return a.reshape((n_seq, tg) + width_shape)[:, :tv]

            if kind == 0:
                (h,) = _norm_mod(x, norm_mix[i], m[0], m[1], tg=tg, out_dtypes=(F32,))
                y, o["rw_shift"], o["rw_wkv"] = _rwkv7(h, rw_state[gi][0], rw_state[gi][1], rw,
                                                       n_seq=n_seq, tg=tg, t_valid=tv)
                w_out = rw_w_o
            elif kind == 1:
                (h,) = _norm_mod(x, norm_mix[i], m[0], m[1], tg=tg, out_dtypes=(BF16,))
                qkv = _matmul(h, sb_w_in.astype(BF16))
                n_h = d // SB_HEAD_DIM
                o["sb_k"] = _to_heads(qkv, 1, n_seq, tg, (n_h, SB_HEAD_DIM))[:, :tv]
                o["sb_v"] = _to_heads(qkv, 2, n_seq, tg, (n_h, SB_HEAD_DIM))[:, :tv]
                if gi == 0:
                    y = _sb_prompt(qkv, n_seq=n_seq, tg=tg)
                else:
                    y = _sb_sample(qkv, cache_sb_k, cache_sb_v, page_table, t_valid=tv)
                w_out = sb_w_out
            elif kind == 2:
                lam_init = 0.8 - 0.6 * math.exp(-0.3 * i)
                lf = da_lambda.astype(F32)
                lam = (jnp.exp(jnp.sum(lf[0] * lf[1])) - jnp.exp(jnp.sum(lf[2] * lf[3])) + lam_init).reshape(1)
                (h,) = _norm_mod(x, norm_mix[i], m[0], m[1], tg=tg, out_dtypes=(BF16,))
                qkv = _matmul(h, da_w_in.astype(BF16))
                qk = _rope_qk(qkv, pos[gi])
                n_h = d // (2 * DA_HEAD_DIM)
                o["da_k"] = _to_heads(qk, 1, n_seq, tg, (n_h, 2, DA_HEAD_DIM))[:, :tv]
                o["da_v"] = rows_of(qkv[:, 2 * d:], (n_h, 2 * DA_HEAD_DIM))
                if gi == 0:
                    y = _da_prompt(qk, qkv, lam, da_subln, n_seq=n_seq, tg=tg, lam_init=lam_init)
                else:
                    y = _da_sample(qk, qkv, cache_da_k, cache_da_v, page_table, lam, da_subln,
                                   t_valid=tv, lam_init=lam_init)
                w_out = da_w_out
            else:
                (h,) = _norm_mod(x, norm_mix[i], m[0], m[1], tg=tg, out_dtypes=(BF16,))
                proj = _matmul(h, gla_w_in.astype(BF16))
                lo = _matmul(_matmul(h, gla_wa1.astype(BF16), out_dtype=BF16), gla_wa2.astype(BF16))
                y, o["gla"] = _gla(proj, lo, gla_ba, gla_norm, gla_state[gi], n_seq=n_seq, tg=tg, t_valid=tv)
                w_out = gla_w_out
            xs[gi] = _matmul(y, w_out.astype(BF16), resid=x, gate=m[2], tg=tg)

        n_g, n_e = moe_w_group.shape[2], moe_w_expert.shape[3]
        router_w = jnp.concatenate([moe_w_group[i]] + [moe_w_expert[i, g] for g in range(n_g)], axis=1)
        router_w = jnp.pad(router_w, ((0, 0), (0, -router_w.shape[1] % LANES)))
        hs, logits = [], []
        for gi, (n_seq, tg, tv) in enumerate(groups):
            m = mods[gi]
            h, lg = _norm_mod(xs[gi], norm_ffn[i], m[3], m[4], tg=tg, out_dtypes=(BF16,), router_w=router_w)
            hs.append(h)
            logits.append(lg)
        xs = _moe_layer(groups, xs, hs, logits, [mods[0][5], mods[1][5]], n_g,
                        moe_w_gate, moe_w_up, moe_w_down, i)

    ys = []
    for gi, (n_seq, tg, tv) in enumerate(groups):
        (y,) = _norm_mod(xs[gi], norm_final, None, None, tg=tg, out_dtypes=(F32,))
        ys.append(y.reshape(n_seq, tg, d)[:, :tv])
    names = ("rw_shift", "rw_wkv", "sb_k", "sb_v", "da_k", "da_v", "gla")
    return (ys[0], ys[1]) + tuple(outs[0][n] for n in names) + tuple(outs[1][n] for n in names)
```

```python
import functools
import math

import jax
import jax.numpy as jnp
from jax import lax
from jax.experimental import pallas as pl
from jax.experimental.pallas import tpu as pltpu

F32 = jnp.float32
BF16 = jnp.bfloat16

LANES = 128
SUBLANES = 8
VMEM_LIMIT_BYTES = 56 * 1024 * 1024

NORM_EPS = 1e-6
ROPE_THETA = 10000.0
RW_HEAD_DIM = 64
RW_GN_EPS = 64e-5
SB_HEAD_DIM = 128
DA_HEAD_DIM = 128
DA_SUBLN_EPS = 1e-5
GLA_HEADS = 4
GLA_TAU = 16.0
GLA_NORM_EPS = 1e-5
MOE_TOPK = 2
SAMPLE_SLOT = 8


def _params(*semantics):
    return pltpu.CompilerParams(dimension_semantics=semantics, vmem_limit_bytes=VMEM_LIMIT_BYTES)


def _split3(x):
    hi = x.astype(BF16)
    r1 = x - hi.astype(F32)
    mid = r1.astype(BF16)
    lo = (r1 - mid.astype(F32)).astype(BF16)
    return hi, mid, lo


_NN = (((1,), (0,)), ((), ()))
_NT = (((1,), (1,)), ((), ()))
_TN = (((0,), (0,)), ((), ()))


def _dot1(a, b, dims=_NN):
    return lax.dot_general(a.astype(BF16), b.astype(BF16), dims, preferred_element_type=F32)


def _dot_exact_rhs(a, b_exact, dims=_NN):
    hi, mid, lo = _split3(a)
    b = b_exact.astype(BF16)
    return (lax.dot_general(hi, b, dims, preferred_element_type=F32)
            + lax.dot_general(mid, b, dims, preferred_element_type=F32)
            + lax.dot_general(lo, b, dims, preferred_element_type=F32))


def _dot_sum16(a, b_exact):
    hi = a.astype(BF16)
    mid = (a - hi.astype(F32)).astype(BF16)
    b = b_exact.astype(BF16)
    return jnp.dot(hi, b, preferred_element_type=F32) + jnp.dot(mid, b, preferred_element_type=F32)


def _dot_exact_lhs(a_exact, b, dims=_NN):
    hi, mid, lo = _split3(b)
    a = a_exact.astype(BF16)
    return (lax.dot_general(a, hi, dims, preferred_element_type=F32)
            + lax.dot_general(a, mid, dims, preferred_element_type=F32)
            + lax.dot_general(a, lo, dims, preferred_element_type=F32))


def _dot3(a, b, dims=_NN):
    ah, am, _ = _split3(a)
    bh, bm, _ = _split3(b)
    d = functools.partial(lax.dot_general, dimension_numbers=dims, preferred_element_type=F32)
    return d(ah, bh) + (d(ah, bm) + d(am, bh))


def _softplus(x):
    return jnp.maximum(x, 0.0) + jnp.log(1.0 + jnp.exp(-jnp.abs(x)))


def _sigmoid(x):
    return 1.0 / (1.0 + jnp.exp(-x))


def _adaln_kernel(c_ref, w_ref, b_ref, o_ref):
    c = c_ref[...]
    s = c * _sigmoid(c)
    o_ref[...] = _dot3(s, w_ref[...]) + b_ref[...]


def _adaln(c_all, ada_w, ada_b):
    n_layers, d, n = ada_w.shape
    rows = c_all.shape[0]
    tn = min(512, n)
    return pl.pallas_call(
        _adaln_kernel,
        out_shape=jax.ShapeDtypeStruct((n_layers, rows, n), F32),
        grid=(n_layers, n // tn),
        in_specs=[
            pl.BlockSpec((rows, d), lambda l, j: (0, 0)),
            pl.BlockSpec((None, d, tn), lambda l, j: (l, 0, j)),
            pl.BlockSpec((None, 1, tn), lambda l, j: (l, 0, j)),
        ],
        out_specs=pl.BlockSpec((None, rows, tn), lambda l, j: (l, 0, j)),
        compiler_params=_params("parallel", "parallel"),
        name="adaln",
    )(c_all, ada_w, ada_b.reshape(n_layers, 1, n))


def _norm_mod_kernel(*refs, eps, modulate, n_out, router):
    x_ref, w_ref = refs[:2]
    pos = 2
    x = x_ref[...]
    y = x * lax.rsqrt(jnp.mean(x * x, axis=-1, keepdims=True) + eps) * w_ref[...]
    if modulate:
        shift_ref, scale_ref = refs[pos:pos + 2]
        pos += 2
        y = y * (1.0 + scale_ref[...]) + shift_ref[...]
    if router:
        wr_ref = refs[pos]
        pos += 1
    outs = refs[pos:]
    for o in outs[:n_out]:
        o[...] = y.astype(o.dtype)
    if router:
        outs[n_out][...] = _dot3(y, wr_ref[...])


def _mod_spec(mod, tm, tg, width):
    g, gr, _ = mod.shape
    if gr == 1:
        per = tg // tm
        return pl.BlockSpec((None, 1, width), lambda i, *_: (i // per, 0, 0))
    assert g == 1 and gr == tm
    return pl.BlockSpec((None, gr, width), lambda i, *_: (0, 0, 0))


def _norm_mod(x, w, shift, scale, *, tg, out_dtypes, eps=NORM_EPS, router_w=None):
    rows, d = x.shape
    tm = min(256, tg) if shift is None or shift.shape[1] == 1 else rows
    in_specs = [pl.BlockSpec((tm, d), lambda i: (i, 0)), pl.BlockSpec((1, d), lambda i: (0, 0))]
    args = [x, w.reshape(1, d)]
    if shift is not None:
        in_specs += [_mod_spec(shift, tm, tg, d), _mod_spec(scale, tm, tg, d)]
        args += [shift, scale]
    out_shape = [jax.ShapeDtypeStruct((rows, d), dt) for dt in out_dtypes]
    out_specs = [pl.BlockSpec((tm, d), lambda i: (i, 0)) for _ in out_dtypes]
    if router_w is not None:
        nr = router_w.shape[1]
        in_specs.append(pl.BlockSpec((d, nr), lambda i: (0, 0)))
        args.append(router_w)
        out_shape.append(jax.ShapeDtypeStruct((rows, nr), F32))
        out_specs.append(pl.BlockSpec((tm, nr), lambda i: (i, 0)))
    return pl.pallas_call(
        functools.partial(_norm_mod_kernel, eps=eps, modulate=shift is not None,
                          n_out=len(out_dtypes), router=router_w is not None),
        out_shape=out_shape,
        grid=(rows // tm,),
        in_specs=in_specs,
        out_specs=out_specs,
        compiler_params=_params("parallel"),
        name="norm_mod",
    )(*args)


def _matmul_kernel(*refs, act, resid):
    a_ref, b_ref = refs[:2]
    o_ref = refs[-1]
    acc = jnp.dot(a_ref[...], b_ref[...], preferred_element_type=F32)
    if act == "tanh":
        acc = jnp.tanh(acc)
    elif act == "sigmoid":
        acc = _sigmoid(acc)
    if resid:
        x_ref, g_ref = refs[2:4]
        acc = x_ref[...] + g_ref[...] * acc
    o_ref[...] = acc.astype(o_ref.dtype)


def _matmul(a, b, *, out_dtype=F32, act=None, resid=None, gate=None, tg=None):
    m, k = a.shape
    n = b.shape[1]
    tm = min(1024, m)
    if gate is not None:
        tm = min(tm, tg) if gate.shape[1] == 1 else m
    tn = min(512, n)
    in_specs = [pl.BlockSpec((tm, k), lambda i, j: (i, 0)), pl.BlockSpec((k, tn), lambda i, j: (0, j))]
    args = [a, b]
    if resid is not None:
        g, gr, _ = gate.shape
        in_specs.append(pl.BlockSpec((tm, tn), lambda i, j: (i, j)))
        if gr == 1:
            per = tg // tm
            in_specs.append(pl.BlockSpec((None, 1, tn), lambda i, j: (i // per, 0, j)))
        else:
            in_specs.append(pl.BlockSpec((None, gr, tn), lambda i, j: (0, 0, j)))
        args += [resid, gate]
    return pl.pallas_call(
        functools.partial(_matmul_kernel, act=act, resid=resid is not None),
        out_shape=jax.ShapeDtypeStruct((m, n), out_dtype),
        grid=(m // tm, n // tn),
        in_specs=in_specs,
        out_specs=pl.BlockSpec((tm, tn), lambda i, j: (i, j)),
        compiler_params=_params("parallel", "parallel"),
        name="matmul",
    )(*args)


def _rw_mix_kernel(h_ref, hp_ref, mu_ref, *o_refs):
    h = h_ref[...]
    xx = hp_ref[...] - h
    for i, o in enumerate(o_refs):
        o[...] = (h + xx * mu_ref[i:i + 1, :]).astype(o.dtype)


def _rw_mix(h, h_prev, mu):
    rows, d = h.shape
    tm = min(256, rows)
    n = mu.shape[0]
    spec = pl.BlockSpec((tm, d), lambda i: (i, 0))
    return pl.pallas_call(
        _rw_mix_kernel,
        out_shape=[jax.ShapeDtypeStruct((rows, d), BF16)] * n,
        grid=(rows // tm,),
        in_specs=[spec, spec, pl.BlockSpec((n, d), lambda i: (0, 0))],
        out_specs=[spec] * n,
        compiler_params=_params("parallel"),
        name="rw_mix",
    )(h, h_prev, mu)


def _rw_recur_kernel(r_ref, k_ref, v_ref, wl_ref, al_ref, g_ref, par_ref, s0_ref,
                     y_ref, sout_ref, s_scr, *, chunk, n_sub, t_valid, precise):
    c = chunk
    n2 = 2 * c
    bi = pl.program_id(2)
    hd = RW_HEAD_DIM

    @pl.when(bi == 0)
    def _():
        s_scr[...] = s0_ref[...]

    par = par_ref[...]
    w0, a0, k_k, k_a, r_k, ln_w, ln_b = (par[i:i + 1, :] for i in range(7))
    lane = lax.broadcasted_iota(jnp.int32, (1, LANES), 1)
    head0 = lane < hd
    lane_r = lax.broadcasted_iota(jnp.int32, (LANES, LANES), 0)
    lane_c = lax.broadcasted_iota(jnp.int32, (LANES, LANES), 1)
    same_head = (lane_r < hd) == (lane_c < hd)
    seg = jnp.where(same_head, 1.0, 0.0)
    ti = lax.broadcasted_iota(jnp.int32, (c, c), 0)
    si = lax.broadcasted_iota(jnp.int32, (c, c), 1)
    tri = jnp.where(ti >= si, 1.0, 0.0)
    rs = lax.broadcasted_iota(jnp.int32, (n2, n2), 0)
    cs = lax.broadcasted_iota(jnp.int32, (n2, n2), 1)
    same_blk = (rs < c) == (cs < c)
    rt = jnp.where(rs < c, rs, rs - c)
    ct = jnp.where(cs < c, cs, cs - c)
    low_s = jnp.logical_and(same_blk, rt > ct)
    low_i = jnp.logical_and(same_blk, rt >= ct)
    eye_n = jnp.where(rs == cs, 1.0, 0.0)
    row_id = lax.broadcasted_iota(jnp.int32, (c, 1), 0)

    def stack(x):
        return jnp.concatenate([x, x], axis=0)

    def stack_heads(x):
        return jnp.concatenate([jnp.where(head0, x, 0.0), jnp.where(head0, 0.0, x)], axis=0)

    def halves(x):
        return jnp.where(head0, x[:c], x[c:])

    subs = range(n_sub)
    rows = [slice(i * c, (i + 1) * c) for i in subs]
    each = lambda f, *xs: [f(*a) for a in zip(*xs)]
    r = [r_ref[rw, :] for rw in rows]
    k = [k_ref[rw, :] for rw in rows]
    v = [v_ref[rw, :] for rw in rows]
    lw = [-jnp.exp(-_softplus(-(w0 + wl_ref[rw, :])) - 0.5) for rw in rows]
    a = [_sigmoid(a0 + al_ref[rw, :]) for rw in rows]
    kk = each(lambda k_: k_ * k_k, k)
    nrm = each(lambda x: jnp.sqrt(_dot_exact_rhs(x * x, seg)), kk)
    kk = each(lambda x, n: x / jnp.maximum(n, 1e-12), kk, nrm)
    kmod = each(lambda k_, a_: k_ * (1.0 + (a_ - 1.0) * k_a), k, a)
    al = each(lambda x: -x, kk)
    be = each(lambda x, a_: x * a_, kk, a)
    kmod_s = kmod
    if t_valid < c * n_sub:
        valid = [(bi * (c * n_sub) + i * c + row_id) < t_valid for i in subs]
        mask = lambda x, m: jnp.where(m, x, 0.0)
        lw, al, be, kmod_s = each(mask, lw, valid), each(mask, al, valid), each(mask, be, valid), each(mask, kmod, valid)

    cum = each(lambda x: _dot_exact_lhs(tri, x), lw)
    p_col = each(lambda x: jnp.exp(jnp.transpose(jnp.broadcast_to(x[c - 1:c, :], (LANES, LANES)))), cum)
    p_i = each(lambda x: jnp.exp(-x), cum)
    p_e = each(lambda x: jnp.exp(x[c - 1:c, :] - x), cum)
    a_t = each(lambda al_, cu, lw_: al_ * jnp.exp(cu - lw_), al, cum, lw)
    b_t = each(lambda x, p: x * p, be, p_i)
    k_t = each(lambda x, p: x * p, kmod_s, p_i)
    r_t = each(lambda x, cu: x * jnp.exp(cu), r, cum)
    b_bar = each(lambda x, p: x * p, be, p_e)
    k_bar = each(lambda x, p: x * p, kmod_s, p_e)

    x_a = each(stack_heads, a_t)
    x_r = each(stack_heads, r_t)
    y_b = each(stack, b_t)
    y_k = each(stack, k_t)
    v_st = each(stack, v)
    mm = _dot3 if precise else _dot1
    l_ab = each(lambda x, y: jnp.where(low_s, mm(x, y, _NT), 0.0), x_a, y_b)
    l_ak = each(lambda x, y: jnp.where(low_s, mm(x, y, _NT), 0.0), x_a, y_k)
    l_rb = each(lambda x, y: jnp.where(low_i, mm(x, y, _NT), 0.0), x_r, y_b)
    l_rk = each(lambda x, y: jnp.where(low_i, mm(x, y, _NT), 0.0), x_r, y_k)
    t_inv = each(lambda l: eye_n + l, l_ab)
    pw = l_ab
    span = 2
    while span < c:
        pw = each(lambda p: mm(p, p), pw)
        t_inv = each(lambda t, p: t + mm(t, p), t_inv, pw)
        span *= 2
    resid = each(lambda t, l: (eye_n - t) + _dot3(l, t), t_inv, l_ab)
    t_inv = each(lambda t, e: t + mm(t, e), t_inv, resid)
    a_hat = each(lambda t, x: halves(mm(t, stack(x))), t_inv, a_t)
    lv = each(mm, l_ak, v_st)
    u_hat = each(lambda t, x: halves(mm(t, x)), t_inv, lv)
    r_hat = each(lambda x, l, ah: x + halves(mm(l, stack(ah))), r_t, l_rb, a_hat)
    y_hat = each(lambda lb, uh, lk, vs: halves(mm(lb, stack(uh)) + mm(lk, vs)), l_rb, u_hat, l_rk, v_st)
    g_low = each(lambda b, ah: jnp.where(same_head, mm(b, ah, _TN), 0.0), b_bar, a_hat)
    z_mat = each(lambda b, uh, kb, v_: jnp.where(same_head, mm(b, uh, _TN) + mm(kb, v_, _TN), 0.0),
                 b_bar, u_hat, k_bar, v)

    s = s_scr[...]
    y = []
    for i in subs:
        y.append(mm(r_hat[i], s) + y_hat[i])
        s = s * p_col[i] + mm(g_low[i], s) + z_mat[i]

    inv_hd = 1.0 / hd
    stat = _dot_exact_rhs if precise else _dot_sum16
    mean = each(lambda x: stat(x, seg) * inv_hd, y)
    yc = each(lambda x, m: x - m, y, mean)
    var = each(lambda x: stat(x * x, seg) * inv_hd, yc)
    bonus = each(lambda r_, km, v_: stat(r_ * km * r_k, seg) * v_, r, kmod, v)
    for i in subs:
        yn = yc[i] * lax.rsqrt(var[i] + RW_GN_EPS) * ln_w + ln_b
        y_ref[rows[i], :] = ((yn + bonus[i]) * g_ref[rows[i], :]).astype(y_ref.dtype)

    s_scr[...] = s

    @pl.when(bi == pl.num_programs(2) - 1)
    def _():
        sout_ref[...] = s


def _rw_recur(r, k, v, wl, al, g, par, s0, *, n_seq, tg, t_valid):
    rows, d = r.shape
    c = min(64, tg)
    n_sub = min(8, tg // c)
    tb = c * n_sub
    n_blocks = tg // tb
    n_pairs = d // LANES
    tile = pl.BlockSpec((tb, LANES), lambda b, p, bi: (b * n_blocks + bi, p))
    state = pl.BlockSpec((None, None, LANES, LANES), lambda b, p, bi: (b, p, 0, 0))
    return pl.pallas_call(
        functools.partial(_rw_recur_kernel, chunk=c, n_sub=n_sub, t_valid=t_valid, precise=tg <= SAMPLE_SLOT),
        out_shape=[jax.ShapeDtypeStruct((rows, d), BF16),
                   jax.ShapeDtypeStruct((n_seq, n_pairs, LANES, LANES), F32)],
        grid=(n_seq, n_pairs, n_blocks),
        in_specs=[tile] * 6 + [pl.BlockSpec((8, LANES), lambda b, p, bi: (0, p)), state],
        out_specs=[tile, state],
        scratch_shapes=[pltpu.VMEM((LANES, LANES), F32)],
        compiler_params=_params("parallel", "parallel", "arbitrary"),
        name="rw_recur",
    )(r, k, v, wl, al, g, par, s0)


def _rw_state_to_pairs(wkv):
    b, h, n, _ = wkv.shape
    st = jnp.swapaxes(wkv, -1, -2).reshape(b, h // 2, 2, n, n)
    z = jnp.zeros_like(st[:, :, 0])
    top = jnp.concatenate([st[:, :, 0], z], axis=-1)
    bot = jnp.concatenate([z, st[:, :, 1]], axis=-1)
    return jnp.concatenate([top, bot], axis=-2)


def _rw_pairs_to_state(s):
    b, p, _, _ = s.shape
    n = RW_HEAD_DIM
    h0 = s[:, :, :n, :n]
    h1 = s[:, :, n:, n:]
    st = jnp.stack([h0, h1], axis=2).reshape(b, 2 * p, n, n)
    return jnp.swapaxes(st, -1, -2)


def _rwkv7(h, shift_prev, wkv_prev, w, *, n_seq, tg, t_valid):
    rows, d = h.shape
    h3 = h.reshape(n_seq, tg, d)
    h_prev = jnp.concatenate([shift_prev[:, None, :], h3[:, :-1]], axis=1).reshape(rows, d)
    xr, xw, xk, xv, xa, xg = _rw_mix(h, h_prev, w["rw_mu"])
    r = _matmul(xr, w["rw_w_r"])
    k = _matmul(xk, w["rw_w_k"])
    v = _matmul(xv, w["rw_w_v"])
    wl = _matmul(_matmul(xw, w["rw_w1"], out_dtype=BF16, act="tanh"), w["rw_w2"])
    al = _matmul(_matmul(xa, w["rw_a1"], out_dtype=BF16), w["rw_a2"])
    g = _matmul(_matmul(xg, w["rw_g1"], out_dtype=BF16, act="sigmoid"), w["rw_g2"])
    par = jnp.stack([w["rw_w0"], w["rw_a0"], w["rw_k_k"], w["rw_k_a"], w["rw_r_k"].reshape(-1),
                     w["rw_ln_w"], w["rw_ln_b"], jnp.zeros((d,), F32)])
    y, s_fin = _rw_recur(r, k, v, wl, al, g, par, _rw_state_to_pairs(wkv_prev),
                         n_seq=n_seq, tg=tg, t_valid=t_valid)
    return y, h3[:, t_valid - 1], _rw_pairs_to_state(s_fin)


def _to_heads_kernel(x_ref, o_ref, *, n_heads, hd):
    tm = x_ref.shape[0]
    for h in range(n_heads):
        o_ref[pl.ds(h, tm, stride=n_heads), :] = x_ref[:, h * hd:(h + 1) * hd]


def _to_heads(x, third, n_seq, tg, head_shape):
    rows = x.shape[0]
    hd = head_shape[-1]
    n_heads = math.prod(head_shape[:-1])
    d = n_heads * hd
    tm = min(256, rows)
    out = pl.pallas_call(
        functools.partial(_to_heads_kernel, n_heads=n_heads, hd=hd),
        out_shape=jax.ShapeDtypeStruct((rows * n_heads, hd), x.dtype),
        grid=(rows // tm,),
        in_specs=[pl.BlockSpec((tm, d), lambda i: (i, third))],
        out_specs=pl.BlockSpec((tm * n_heads, hd), lambda i: (i, 0)),
        compiler_params=_params("parallel"),
        name="to_heads",
    )(x)
    return out.reshape((n_seq, tg) + tuple(head_shape))


def _sb_tile(z, mask, later_carry, u_mat):
    lk0 = -_softplus(z)
    lk = lk0 if mask is None else jnp.where(mask, lk0, 0.0)
    later = _dot_sum16(lk, u_mat)
    w = jnp.exp(z + lk0 + later + later_carry)
    if mask is not None:
        w = jnp.where(mask, w, 0.0)
    return w, later_carry + later[:, 0:1] + lk[:, 0:1]


def _suffix_matrix(n):
    j = lax.broadcasted_iota(jnp.int32, (n, n), 0)
    k = lax.broadcasted_iota(jnp.int32, (n, n), 1)
    return jnp.where(j > k, 1.0, 0.0)


def _sb_prompt_kernel(q_ref, k_ref, v_ref, o_ref, *, bq, n_par):
    qi = pl.program_id(2)
    hd = SB_HEAD_DIM
    scale = hd ** -0.5
    heads = range(n_par)
    cols = [slice(h * hd, (h + 1) * hd) for h in heads]
    q = [q_ref[:, c].astype(BF16) for c in cols]
    u_mat = _suffix_matrix(bq)
    row = lax.broadcasted_iota(jnp.int32, (bq, bq), 0)
    col = lax.broadcasted_iota(jnp.int32, (bq, bq), 1)

    def block(kb, state, mask):
        start = pl.multiple_of(kb * bq, bq)
        z = [lax.dot_general(q[h], k_ref[pl.ds(start, bq), cols[h]].astype(BF16), _NT,
                             preferred_element_type=F32) * scale for h in heads]
        tiles = [_sb_tile(z[h], mask, state[h][1], u_mat) for h in heads]
        return tuple(
            (state[h][0] + jnp.dot(tiles[h][0].astype(BF16), v_ref[pl.ds(start, bq), cols[h]].astype(BF16),
                                   preferred_element_type=F32), tiles[h][1]) for h in heads)

    init = tuple((jnp.zeros((bq, hd), F32), jnp.zeros((bq, 1), F32)) for _ in heads)
    state = block(qi, init, col < row)
    state = lax.fori_loop(1, qi + 1, lambda j, st: block(qi - j, st, None), state)
    for h in heads:
        o_ref[:, cols[h]] = state[h][0].astype(o_ref.dtype)


def _sb_prompt(qkv, *, n_seq, tg):
    rows, d3 = qkv.shape
    d = d3 // 3
    hd = SB_HEAD_DIM
    n_heads = d // hd
    n_par = 2 if n_heads % 2 == 0 else 1
    hw = n_par * hd
    n_hb = n_heads // n_par
    bq = min(256, tg)
    nq = tg // bq
    return pl.pallas_call(
        functools.partial(_sb_prompt_kernel, bq=bq, n_par=n_par),
        out_shape=jax.ShapeDtypeStruct((rows, d), BF16),
        grid=(n_seq, n_hb, nq),
        in_specs=[
            pl.BlockSpec((bq, hw), lambda b, h, qi: (b * nq + qi, h)),
            pl.BlockSpec((tg, hw), lambda b, h, qi: (b, n_hb + h)),
            pl.BlockSpec((tg, hw), lambda b, h, qi: (b, 2 * n_hb + h)),
        ],
        out_specs=pl.BlockSpec((bq, hw), lambda b, h, qi: (b * nq + qi, h)),
        compiler_params=_params("parallel", "parallel", "arbitrary"),
        name="sb_prompt",
    )(qkv, qkv, qkv)


def _sb_sample_kernel(pt_ref, q_ref, kn_ref, vn_ref, kc_ref, vc_ref, o_ref, acc_scr, carry_scr,
                      *, n_heads, t_valid, page):
    j = pl.program_id(1)
    hd = SB_HEAD_DIM
    slot = SAMPLE_SLOT
    scale = hd ** -0.5
    n_rows = n_heads * slot

    def tile(get_k, get_v, nk, first):
        q = q_ref[...].astype(BF16)
        z = jnp.concatenate(
            [lax.dot_general(q[:, h * hd:(h + 1) * hd], get_k(h).astype(BF16), _NT, preferred_element_type=F32)
             for h in range(n_heads)], axis=0) * scale
        if first:
            row = lax.broadcasted_iota(jnp.int32, (n_rows, nk), 0)
            col = lax.broadcasted_iota(jnp.int32, (n_rows, nk), 1)
            mask = col < jnp.minimum(row & (slot - 1), t_valid)
            carry = jnp.zeros((n_rows, 1), F32)
        else:
            mask = None
            carry = carry_scr[:, 0:1]
        w, carry = _sb_tile(z, mask, carry, _suffix_matrix(nk))
        carry_scr[...] = jnp.broadcast_to(carry, (n_rows, LANES))
        wb = w.astype(BF16)
        for h in range(n_heads):
            pv = jnp.dot(wb[h * slot:(h + 1) * slot], get_v(h).astype(BF16), preferred_element_type=F32)
            sl = slice(h * hd, (h + 1) * hd)
            acc_scr[:, sl] = pv if first else acc_scr[:, sl] + pv

    @pl.when(j == 0)
    def _():
        tile(lambda h: kn_ref[:, h * hd:(h + 1) * hd], lambda h: vn_ref[:, h * hd:(h + 1) * hd], slot, True)

    @pl.when(j > 0)
    def _():
        tile(lambda h: kc_ref[pl.ds(h, page, stride=n_heads), :],
             lambda h: vc_ref[pl.ds(h, page, stride=n_heads), :], page, False)

    @pl.when(j == pl.num_programs(1) - 1)
    def _():
        o_ref[...] = acc_scr[...].astype(o_ref.dtype)


def _sb_sample(qkv, cache_k, cache_v, page_table, *, t_valid):
    n_seq, n_pages = page_table.shape
    rows, d3 = qkv.shape
    d = d3 // 3
    n_phys, page, n_heads, hd = cache_k.shape
    slot = SAMPLE_SLOT

    ck = cache_k.reshape(n_phys, page * n_heads, hd)
    cv = cache_v.reshape(n_phys, page * n_heads, hd)
    page_spec = pl.BlockSpec((None, page * n_heads, hd), lambda s, j, pt: (pt[s, n_pages - jnp.maximum(j, 1)], 0, 0))

    grid_spec = pltpu.PrefetchScalarGridSpec(
        num_scalar_prefetch=1,
        grid=(n_seq, n_pages + 1),
        in_specs=[
            pl.BlockSpec((slot, d), lambda s, j, pt: (s, 0)),
            pl.BlockSpec((slot, d), lambda s, j, pt: (s, 1)),
            pl.BlockSpec((slot, d), lambda s, j, pt: (s, 2)),
            page_spec,
            page_spec,
        ],
        out_specs=pl.BlockSpec((slot, d), lambda s, j, pt: (s, 0)),
        scratch_shapes=[pltpu.VMEM((slot, d), F32), pltpu.VMEM((n_heads * slot, LANES), F32)],
    )
    return pl.pallas_call(
        functools.partial(_sb_sample_kernel, n_heads=n_heads, t_valid=t_valid, page=page),
        out_shape=jax.ShapeDtypeStruct((rows, d), BF16),
        grid_spec=grid_spec,
        compiler_params=_params("parallel", "arbitrary"),
        name="sb_sample",
    )(page_table, qkv, qkv, qkv, ck, cv)


def _rope_kernel(x_ref, cs_ref, sn_ref, o_ref):
    cs = cs_ref[...]
    sn = sn_ref[...]
    for c in range(x_ref.shape[1] // LANES):
        x = x_ref[:, c * LANES:(c + 1) * LANES]
        o_ref[:, c * LANES:(c + 1) * LANES] = x * cs + pltpu.roll(x, LANES // 2, 1) * sn


def _rope_tables(pos):
    half = DA_HEAD_DIM // 2
    inv_freq = ROPE_THETA ** (-jnp.arange(half, dtype=F32) / half)
    ang = pos.astype(F32)[:, None] * inv_freq[None, :]
    cos, sin = jnp.cos(ang), jnp.sin(ang)
    return jnp.concatenate([cos, cos], axis=1), jnp.concatenate([-sin, sin], axis=1)


def _rope_qk(qkv, pos_rows):
    rows, d3 = qkv.shape
    d2 = d3 // 3 * 2
    cs, sn = _rope_tables(pos_rows)
    tt = cs.shape[0]
    tm = min(256, tt)
    tn = min(1024, d2)
    per = tt // tm
    return pl.pallas_call(
        _rope_kernel,
        out_shape=jax.ShapeDtypeStruct((rows, d2), F32),
        grid=(rows // tm, d2 // tn),
        in_specs=[
            pl.BlockSpec((tm, tn), lambda i, j: (i, j)),
            pl.BlockSpec((tm, LANES), lambda i, j: (i % per, 0)),
            pl.BlockSpec((tm, LANES), lambda i, j: (i % per, 0)),
        ],
        out_specs=pl.BlockSpec((tm, tn), lambda i, j: (i, j)),
        compiler_params=_params("parallel", "parallel"),
        name="rope",
    )(qkv, cs, sn)


def _softmax_tile(s, mask, m_prev, l_prev):
    if mask is not None:
        s = jnp.where(mask, s, -jnp.inf)
    m_new = jnp.maximum(m_prev, jnp.max(s, axis=-1, keepdims=True))
    p = jnp.exp(s - m_new)
    alpha = jnp.exp(m_prev - m_new)
    return p, m_new, alpha, alpha * l_prev + jnp.sum(p, axis=-1, keepdims=True)


def _da_finish(acc1, l1, acc2, l2, lam, subln, lam_init):
    o = acc1 / l1 - lam * (acc2 / l2)
    o = o * lax.rsqrt(jnp.mean(o * o, axis=-1, keepdims=True) + DA_SUBLN_EPS) * subln
    return o * (1.0 - lam_init)


def _da_prompt_kernel(lam_ref, q_ref, k_ref, v_ref, sub_ref, o_ref, *, bq, lam_init):
    qi = pl.program_id(2)
    hd = DA_HEAD_DIM
    scale = hd ** -0.5
    q1 = q_ref[:, :hd].astype(BF16)
    q2 = q_ref[:, hd:].astype(BF16)
    row = lax.broadcasted_iota(jnp.int32, (bq, bq), 0)
    col = lax.broadcasted_iota(jnp.int32, (bq, bq), 1)

    def block(kb, state, mask):
        m1, l1, acc1, m2, l2, acc2 = state
        start = pl.multiple_of(kb * bq, bq)
        k_blk = k_ref[pl.ds(start, bq), :].astype(BF16)
        v_blk = v_ref[pl.ds(start, bq), :].astype(BF16)
        s1 = lax.dot_general(q1, k_blk[:, :hd], _NT, preferred_element_type=F32) * scale
        s2 = lax.dot_general(q2, k_blk[:, hd:], _NT, preferred_element_type=F32) * scale
        p1, m1, a1, l1 = _softmax_tile(s1, mask, m1, l1)
        p2, m2, a2, l2 = _softmax_tile(s2, mask, m2, l2)
        acc1 = a1 * acc1 + jnp.dot(p1.astype(BF16), v_blk, preferred_element_type=F32)
        acc2 = a2 * acc2 + jnp.dot(p2.astype(BF16), v_blk, preferred_element_type=F32)
        return m1, l1, acc1, m2, l2, acc2

    neg = jnp.full((bq, 1), -jnp.inf, F32)
    zero = jnp.zeros((bq, 1), F32)
    zacc = jnp.zeros((bq, 2 * hd), F32)
    state = lax.fori_loop(0, qi, lambda kb, st: block(kb, st, None), (neg, zero, zacc, neg, zero, zacc))
    _, l1, acc1, _, l2, acc2 = block(qi, state, col <= row)
    o_ref[...] = _da_finish(acc1, l1, acc2, l2, lam_ref[0], sub_ref[...], lam_init).astype(o_ref.dtype)


def _da_prompt(qk, qkv, lam, subln, *, n_seq, tg, lam_init):
    rows, d2 = qk.shape
    d = d2 // 2
    hw = 2 * DA_HEAD_DIM
    n_heads = d // hw
    bq = min(256, tg)
    nq = tg // bq
    return pl.pallas_call(
        functools.partial(_da_prompt_kernel, bq=bq, lam_init=lam_init),
        out_shape=jax.ShapeDtypeStruct((rows, d), BF16),
        grid=(n_seq, n_heads, nq),
        in_specs=[
            pl.BlockSpec(memory_space=pltpu.SMEM),
            pl.BlockSpec((bq, hw), lambda b, h, qi: (b * nq + qi, h)),
            pl.BlockSpec((tg, hw), lambda b, h, qi: (b, n_heads + h)),
            pl.BlockSpec((tg, hw), lambda b, h, qi: (b, 2 * n_heads + h)),
            pl.BlockSpec((1, hw), lambda b, h, qi: (0, 0)),
        ],
        out_specs=pl.BlockSpec((bq, hw), lambda b, h, qi: (b * nq + qi, h)),
        compiler_params=_params("parallel", "parallel", "arbitrary"),
        name="da_prompt",
    )(lam, qk, qk, qkv, subln.reshape(1, hw))


def _da_sample_kernel(pt_ref, q_ref, kn_ref, vn_ref, kc_ref, vlo_ref, vhi_ref, lam_ref, sub_ref, o_ref,
                      m_scr, l_scr, acc_scr, *, n_heads, t_valid, page, lam_init):
    j = pl.program_id(1)
    hd = DA_HEAD_DIM
    hw = 2 * hd
    slot = SAMPLE_SLOT
    scale = hd ** -0.5
    n_rows = 2 * n_heads * slot

    def tile(get_k, get_v, nk, first):
        q = q_ref[...].astype(BF16)
        s = jnp.concatenate(
            [lax.dot_general(q[:, i * hd:(i + 1) * hd], get_k(i // 2, i % 2).astype(BF16), _NT,
                             preferred_element_type=F32) for i in range(2 * n_heads)], axis=0) * scale
        if first:
            row = lax.broadcasted_iota(jnp.int32, (n_rows, nk), 0)
            col = lax.broadcasted_iota(jnp.int32, (n_rows, nk), 1)
            mask = col <= jnp.minimum(row & (slot - 1), t_valid - 1)
            m_prev = jnp.full((n_rows, 1), -jnp.inf, F32)
            l_prev = jnp.zeros((n_rows, 1), F32)
        else:
            mask = None
            m_prev = m_scr[:, 0:1]
            l_prev = l_scr[:, 0:1]
        p, m_new, alpha, l_new = _softmax_tile(s, mask, m_prev, l_prev)
        m_scr[...] = jnp.broadcast_to(m_new, (n_rows, LANES))
        l_scr[...] = jnp.broadcast_to(l_new, (n_rows, LANES))
        pb = p.astype(BF16)
        for h in range(n_heads):
            vb = get_v(h).astype(BF16)
            sl = slice(h * hw, (h + 1) * hw)
            for c in range(2):
                rows = slice((2 * h + c) * slot, (2 * h + c + 1) * slot)
                pv = jnp.dot(pb[rows], vb, preferred_element_type=F32)
                acc_scr[c, :, sl] = pv if first else alpha[rows] * acc_scr[c, :, sl] + pv

    @pl.when(j == 0)
    def _():
        tile(lambda h, c: kn_ref[:, (2 * h + c) * hd:(2 * h + c + 1) * hd],
             lambda h: vn_ref[:, h * hw:(h + 1) * hw], slot, True)

    @pl.when(j > 0)
    def _():
        tile(lambda h, c: kc_ref[pl.ds(2 * h + c, page, stride=2 * n_heads), :],
             lambda h: jnp.concatenate([vlo_ref[pl.ds(h, page, stride=n_heads), :],
                                        vhi_ref[pl.ds(h, page, stride=n_heads), :]], axis=1), page, False)

    @pl.when(j == pl.num_programs(1) - 1)
    def _():
        l_all = l_scr[:, 0:1]
        for h in range(n_heads):
            sl = slice(h * hw, (h + 1) * hw)
            r1 = slice(2 * h * slot, (2 * h + 1) * slot)
            r2 = slice((2 * h + 1) * slot, (2 * h + 2) * slot)
            o = _da_finish(acc_scr[0, :, sl], l_all[r1], acc_scr[1, :, sl], l_all[r2],
                           lam_ref[0], sub_ref[...], lam_init)
            o_ref[:, sl] = o.astype(o_ref.dtype)


def _da_sample(qk, qkv, cache_k, cache_v, page_table, lam, subln, *, t_valid, lam_init):
    rows, d2 = qk.shape
    d = d2 // 2
    hd = DA_HEAD_DIM
    hw = 2 * hd
    n_heads = d // hw
    n_seq, n_pages = page_table.shape
    n_phys, page = cache_k.shape[:2]
    ck = cache_k.reshape(n_phys, page * 2 * n_heads, hd)
    cv = cache_v.reshape(n_phys, page * n_heads, hw)
    slot = SAMPLE_SLOT

    def page_map(half):
        return lambda s, j, pt: (pt[s, n_pages - jnp.maximum(j, 1)], 0, half)

    grid_spec = pltpu.PrefetchScalarGridSpec(
        num_scalar_prefetch=1,
        grid=(n_seq, n_pages + 1),
        in_specs=[
            pl.BlockSpec((slot, d), lambda s, j, pt: (s, 0)),
            pl.BlockSpec((slot, d), lambda s, j, pt: (s, 1)),
            pl.BlockSpec((slot, d), lambda s, j, pt: (s, 2)),
            pl.BlockSpec((None, page * 2 * n_heads, hd), page_map(0)),
            pl.BlockSpec((None, page * n_heads, hd), page_map(0)),
            pl.BlockSpec((None, page * n_heads, hd), page_map(1)),
            pl.BlockSpec(memory_space=pltpu.SMEM),
            pl.BlockSpec((1, hw), lambda s, j, pt: (0, 0)),
        ],
        out_specs=pl.BlockSpec((slot, d), lambda s, j, pt: (s, 0)),
        scratch_shapes=[pltpu.VMEM((2 * n_heads * slot, LANES), F32),
                        pltpu.VMEM((2 * n_heads * slot, LANES), F32),
                        pltpu.VMEM((2, slot, d), F32)],
    )
    return pl.pallas_call(
        functools.partial(_da_sample_kernel, n_heads=n_heads, t_valid=t_valid, page=page, lam_init=lam_init),
        out_shape=jax.ShapeDtypeStruct((rows, d), BF16),
        grid_spec=grid_spec,
        compiler_params=_params("parallel", "arbitrary"),
        name="da_sample",
    )(page_table, qk, qk, qkv, ck, cv, cv, lam, subln.reshape(1, hw))


def _gla_kernel(q_ref, k_ref, v_ref, gate_ref, lo_ref, ba_ref, nw_ref, s0_ref, o_ref, sout_ref, s_scr,
                *, tb, c, tg, t_valid, dk):
    ci = pl.program_id(2)

    @pl.when(ci == 0)
    def _():
        s_scr[...] = s0_ref[...]

    ti = lax.broadcasted_iota(jnp.int32, (c, c), 0)
    si = lax.broadcasted_iota(jnp.int32, (c, c), 1)
    tri = jnp.where(ti >= si, 1.0, 0.0)
    row_id = lax.broadcasted_iota(jnp.int32, (c, 1), 0)
    ones_cl = jnp.ones((c, LANES), F32)
    ba = ba_ref[...]
    nw = nw_ref[...]
    for sub in range(tb // c):
        rows = slice(sub * c, (sub + 1) * c)
        q = q_ref[rows, :] * (dk ** -0.5)
        k = k_ref[rows, :]
        v = v_ref[rows, :]
        la = -_softplus(-(lo_ref[rows, :] + ba)) / GLA_TAU
        if t_valid < tg:
            valid = (ci * tb + sub * c + row_id) < t_valid
            la = jnp.where(valid, la, 0.0)
            k = jnp.where(valid, k, 0.0)
        cum = _dot_exact_lhs(tri, la)
        cum_end = cum[c - 1:c, :]
        s = s_scr[...]
        o = _dot1(q * jnp.exp(cum), s)
        att = jnp.zeros((c, c), F32)
        for j in range(c):
            e = jnp.exp(jnp.where(row_id >= j, cum - cum[j:j + 1, :], -jnp.inf))
            p = jnp.sum(q * k[j:j + 1, :] * e, axis=-1, keepdims=True)
            att = jnp.where(si == j, p, att)
        o = o + _dot1(att, v)
        dec = jnp.exp(_dot_exact_rhs(la, ones_cl, _TN)[:, 0:1])
        s_scr[...] = s * dec + _dot1(k * jnp.exp(cum_end - cum), v, _TN)
        on = o * lax.rsqrt(jnp.mean(o * o, axis=-1, keepdims=True) + GLA_NORM_EPS) * nw
        g = gate_ref[rows, :]
        o_ref[rows, :] = (on * (g * _sigmoid(g))).astype(o_ref.dtype)

    @pl.when(ci == pl.num_programs(2) - 1)
    def _():
        sout_ref[...] = s_scr[...]


def _gla(proj, lo, ba, norm_w, s0, *, n_seq, tg, t_valid):
    rows = proj.shape[0]
    key = lo.shape[1]
    d = (proj.shape[1] - 2 * key) // 2
    nh = GLA_HEADS
    dk, dv = key // nh, d // nh
    tb = min(128, tg)
    c = min(32, tb)
    nc = tg // tb
    kv = key // dv
    return pl.pallas_call(
        functools.partial(_gla_kernel, tb=tb, c=c, tg=tg, t_valid=t_valid, dk=dk),
        out_shape=[jax.ShapeDtypeStruct((rows, d), BF16), jax.ShapeDtypeStruct((n_seq, nh, dk, dv), F32)],
        grid=(n_seq, nh, nc),
        in_specs=[
            pl.BlockSpec((tb, dk), lambda b, h, ci: (b * nc + ci, h)),
            pl.BlockSpec((tb, dk), lambda b, h, ci: (b * nc + ci, nh + h)),
            pl.BlockSpec((tb, dv), lambda b, h, ci: (b * nc + ci, 2 * kv + h)),
            pl.BlockSpec((tb, dv), lambda b, h, ci: (b * nc + ci, 2 * kv + nh + h)),
            pl.BlockSpec((tb, dk), lambda b, h, ci: (b * nc + ci, h)),
            pl.BlockSpec((1, dk), lambda b, h, ci: (0, h)),
            pl.BlockSpec((1, dv), lambda b, h, ci: (0, 0)),
            pl.BlockSpec((None, None, dk, dv), lambda b, h, ci: (b, h, 0, 0)),
        ],
        out_specs=[pl.BlockSpec((tb, dv), lambda b, h, ci: (b * nc + ci, h)),
                   pl.BlockSpec((None, None, dk, dv), lambda b, h, ci: (b, h, 0, 0))],
        scratch_shapes=[pltpu.VMEM((dk, dv), F32)],
        compiler_params=_params("parallel", "parallel", "arbitrary"),
        name="gla",
    )(proj, proj, proj, proj, lo, ba.reshape(1, key), norm_w.reshape(1, dv), s0)


MOE_BLOCK_ROWS = 256


def _moe_route(logits, n_groups, n_experts):
    g_logits = logits[:, :n_groups]
    grp = jnp.argmax(g_logits, axis=-1)
    g_w = jnp.take_along_axis(jax.nn.softmax(g_logits, axis=-1), grp[:, None], axis=-1)
    e_logits = logits[:, n_groups:n_groups + n_groups * n_experts].reshape(-1, n_groups, n_experts)
    e_logits = jnp.take_along_axis(e_logits, grp[:, None, None], axis=1)[:, 0]
    top_val, top_idx = lax.top_k(e_logits, MOE_TOPK)
    gate = g_w * jax.nn.softmax(top_val, axis=-1)
    return (grp[:, None] * n_experts + top_idx).astype(jnp.int32), gate


def _moe_plan(expert_id, n_exp):
    tm = MOE_BLOCK_ROWS
    n_tok, top_k = expert_id.shape
    n_assign = n_tok * top_k
    e_flat = expert_id.reshape(-1)
    order = jnp.argsort(e_flat)
    inv = jnp.argsort(order)
    counts = jnp.bincount(e_flat, length=n_exp)
    start = jnp.cumsum(counts) - counts
    padded = (counts + tm - 1) // tm * tm
    pad_end = jnp.cumsum(padded)
    pad_start = pad_end - padded
    pos = (inv + (pad_start - start)[e_flat]).astype(jnp.int32).reshape(n_tok, top_k)
    n_blocks = -(-n_assign // tm) + n_exp
    block_expert = jnp.minimum(jnp.searchsorted(pad_end, jnp.arange(n_blocks) * tm, side="right"), n_exp - 1)
    row = jnp.arange(n_blocks * tm)
    e_row = jnp.repeat(block_expert, tm)
    rank = row - pad_start[e_row]
    src = order[jnp.clip(rank + start[e_row], 0, n_assign - 1)] // top_k
    src_tok = jnp.where(rank < counts[e_row], src, 0).astype(jnp.int32)
    n_used = (pad_end[-1] // tm).astype(jnp.int32).reshape(1)
    return src_tok, pos, block_expert.astype(jnp.int32), n_used


def _new_expert(be_ref, b):
    return jnp.logical_or(b == 0, be_ref[b] != be_ref[jnp.maximum(b - 1, 0)])


def _moe_up_kernel(be_ref, nu_ref, x_ref, wg_ref, wu_ref, h_ref, wg_bf, wu_bf):
    b = pl.program_id(0)
    used = b < nu_ref[0]

    @pl.when(jnp.logical_and(used, _new_expert(be_ref, b)))
    def _():
        wg_bf[...] = wg_ref[...].astype(BF16)
        wu_bf[...] = wu_ref[...].astype(BF16)

    @pl.when(used)
    def _():
        x = x_ref[...].astype(BF16)
        g = jnp.dot(x, wg_bf[...], preferred_element_type=F32)
        u = jnp.dot(x, wu_bf[...], preferred_element_type=F32)
        h_ref[...] = (g * _sigmoid(g) * u).astype(h_ref.dtype)

    @pl.when(jnp.logical_not(used))
    def _():
        h_ref[...] = jnp.zeros_like(h_ref)


def _moe_down_kernel(be_ref, nu_ref, h_ref, wd_ref, o_ref, wd_bf):
    b = pl.program_id(0)
    used = b < nu_ref[0]

    @pl.when(jnp.logical_and(used, _new_expert(be_ref, b)))
    def _():
        wd_bf[...] = wd_ref[...].astype(BF16)

    @pl.when(used)
    def _():
        o_ref[...] = jnp.dot(h_ref[...], wd_bf[...], preferred_element_type=F32)

    @pl.when(jnp.logical_not(used))
    def _():
        o_ref[...] = jnp.zeros_like(o_ref)


def _moe_experts(rows, block_expert, n_used, w_gate, w_up, w_down, layer):
    tm = MOE_BLOCK_ROWS
    n_rows, d = rows.shape
    f = w_gate.shape[3]
    n_blocks = n_rows // tm
    up_spec = pltpu.PrefetchScalarGridSpec(
        num_scalar_prefetch=2,
        grid=(n_blocks,),
        in_specs=[
            pl.BlockSpec((tm, d), lambda b, be, nu: (b, 0)),
            pl.BlockSpec((None, None, d, f), lambda b, be, nu: (layer, be[b], 0, 0)),
            pl.BlockSpec((None, None, d, f), lambda b, be, nu: (layer, be[b], 0, 0)),
        ],
        out_specs=pl.BlockSpec((tm, f), lambda b, be, nu: (b, 0)),
        scratch_shapes=[pltpu.VMEM((d, f), BF16), pltpu.VMEM((d, f), BF16)],
    )
    h = pl.pallas_call(
        _moe_up_kernel,
        out_shape=jax.ShapeDtypeStruct((n_rows, f), BF16),
        grid_spec=up_spec,
        compiler_params=_params("arbitrary"),
        name="moe_up",
    )(block_expert, n_used, rows, w_gate, w_up)
    down_spec = pltpu.PrefetchScalarGridSpec(
        num_scalar_prefetch=2,
        grid=(n_blocks,),
        in_specs=[
            pl.BlockSpec((tm, f), lambda b, be, nu: (b, 0)),
            pl.BlockSpec((None, None, f, d), lambda b, be, nu: (layer, be[b], 0, 0)),
        ],
        out_specs=pl.BlockSpec((tm, d), lambda b, be, nu: (b, 0)),
        scratch_shapes=[pltpu.VMEM((f, d), BF16)],
    )
    return pl.pallas_call(
        _moe_down_kernel,
        out_shape=jax.ShapeDtypeStruct((n_rows, d), F32),
        grid_spec=down_spec,
        compiler_params=_params("arbitrary"),
        name="moe_down",
    )(block_expert, n_used, h, w_down)


def _moe_combine_kernel(x_ref, o0_ref, o1_ref, g_ref, m_ref, y_ref):
    g = g_ref[...]
    y = o0_ref[...] * g[:, 0:1] + o1_ref[...] * g[:, 1:2]
    y_ref[...] = x_ref[...] + m_ref[...] * y


def _moe_combine(x, o0, o1, gates, gmod, *, tg):
    rows, d = x.shape
    tm = min(256, tg) if gmod.shape[1] == 1 else rows
    tile = pl.BlockSpec((tm, d), lambda i: (i, 0))
    return pl.pallas_call(
        _moe_combine_kernel,
        out_shape=jax.ShapeDtypeStruct((rows, d), F32),
        grid=(rows // tm,),
        in_specs=[tile, tile, tile, pl.BlockSpec((tm, MOE_TOPK), lambda i: (i, 0)), _mod_spec(gmod, tm, tg, d)],
        out_specs=tile,
        compiler_params=_params("parallel"),
        name="moe_combine",
    )(x, o0, o1, gates, gmod)


def _moe_layer(groups, xs, hs, logits, gmods, n_groups, w_gate, w_up, w_down, layer):
    n_experts = w_gate.shape[1] // n_groups
    d = xs[0].shape[1]
    real_h, real_l = [], []
    for grp, h, lg in zip(groups, hs, logits):
        n_seq, tg, tv = grp
        real_h.append(h.reshape(n_seq, tg, d)[:, :tv].reshape(n_seq * tv, d))
        real_l.append(lg.reshape(n_seq, tg, -1)[:, :tv].reshape(n_seq * tv, -1))
    h_all = jnp.concatenate(real_h, axis=0)
    expert_id, gate = _moe_route(jnp.concatenate(real_l, axis=0), n_groups, n_experts)
    src_tok, pos, block_expert, n_used = _moe_plan(expert_id, n_groups * n_experts)
    out = _moe_experts(h_all[src_tok], block_expert, n_used, w_gate, w_up, w_down, layer)
    new_xs = []
    start = 0
    for grp, x, gmod in zip(groups, xs, gmods):
        n_seq, tg, tv = grp
        n = n_seq * tv
        rows = pos[start:start + n]
        parts = []
        for a in (out[rows[:, 0]], out[rows[:, 1]], gate[start:start + n]):
            a = a.reshape(n_seq, tv, -1)
            if tv < tg:
                a = jnp.pad(a, ((0, 0), (0, tg - tv), (0, 0)))
            parts.append(a.reshape(n_seq * tg, -1))
        new_xs.append(_moe_combine(x, parts[0], parts[1], parts[2], gmod, tg=tg))
        start += n
    return new_xs


def kernel(x_prompt, x_sample, c_prompt, c_sample, state_rwkv_shift, state_rwkv_wkv, cache_sb_k, cache_sb_v, cache_da_k, cache_da_v, state_gla, page_table, ada_w, ada_b, norm_mix, norm_ffn, norm_final, rw_mu, rw_w_r, rw_w_k, rw_w_v, rw_w_o, rw_w0, rw_w1, rw_w2, rw_a0, rw_a1, rw_a2, rw_g1, rw_g2, rw_k_k, rw_k_a, rw_r_k, rw_ln_w, rw_ln_b, sb_w_in, sb_w_out, da_w_in, da_w_out, da_lambda, da_subln, gla_w_in, gla_wa1, gla_wa2, gla_ba, gla_norm, gla_w_out, moe_w_group, moe_w_expert, moe_w_gate, moe_w_up, moe_w_down):
    n_p, t_p, d = x_prompt.shape
    n_s, t_s, _ = x_sample.shape
    depth = ada_w.shape[0]
    slot = SAMPLE_SLOT
    n_pages, page = page_table.shape[1], cache_sb_k.shape[1]
    past_len = n_pages * page
    groups = [(n_p, t_p, t_p), (n_s, slot, t_s)]

    n_c = n_p + n_s
    c_all = jnp.concatenate([c_prompt, c_sample], axis=0)
    c_all = jnp.pad(c_all, ((0, -n_c % SUBLANES), (0, 0)))
    mod = _adaln(c_all, ada_w, ada_b)

    xs = [x_prompt.reshape(n_p * t_p, d),
          jnp.pad(x_sample, ((0, 0), (0, slot - t_s), (0, 0))).reshape(n_s * slot, d)]
    rw = dict(rw_mu=rw_mu, rw_w0=rw_w0, rw_a0=rw_a0, rw_k_k=rw_k_k, rw_k_a=rw_k_a, rw_r_k=rw_r_k,
              rw_ln_w=rw_ln_w, rw_ln_b=rw_ln_b)
    for name, val in (("rw_w_r", rw_w_r), ("rw_w_k", rw_w_k), ("rw_w_v", rw_w_v), ("rw_w1", rw_w1),
                      ("rw_w2", rw_w2), ("rw_a1", rw_a1), ("rw_a2", rw_a2), ("rw_g1", rw_g1), ("rw_g2", rw_g2)):
        rw[name] = val.astype(BF16)
    pos = [jnp.arange(t_p, dtype=jnp.int32), jnp.tile(past_len + jnp.arange(slot, dtype=jnp.int32), n_s)]
    rw_state = [(jnp.zeros((n_p, d), F32), jnp.zeros((n_p, d // RW_HEAD_DIM, RW_HEAD_DIM, RW_HEAD_DIM), F32)),
                (state_rwkv_shift, state_rwkv_wkv)]
    gla_state = [jnp.zeros((n_p,) + state_gla.shape[1:], F32), state_gla]
    outs = [dict(), dict()]

    for i in range(depth):
        mods = []
        for gi, (n_seq, tg, tv) in enumerate(groups):
            rows = mod[i, :n_p] if gi == 0 else mod[i, n_p:n_c]
            six = [rows[:, j * d:(j + 1) * d] for j in range(6)]
            if gi == 0:
                mods.append([m[:, None, :] for m in six])
            else:
                mods.append([jnp.repeat(m, tg, axis=0)[None] for m in six])
        kind = i % 4
        for gi, (n_seq, tg, tv) in enumerate(groups):
            m = mods[gi]
            x = xs[gi]
            o = outs[gi]

            def rows_of(a, width_shape):
                return a.reshape((n_seq, tg) + width_shape)[:, :tv]

            if kind == 0:
                (h,) = _norm_mod(x, norm_mix[i], m[0], m[1], tg=tg, out_dtypes=(F32,))
                y, o["rw_shift"], o["rw_wkv"] = _rwkv7(h, rw_state[gi][0], rw_state[gi][1], rw,
                                                       n_seq=n_seq, tg=tg, t_valid=tv)
                w_out = rw_w_o
            elif kind == 1:
                (h,) = _norm_mod(x, norm_mix[i], m[0], m[1], tg=tg, out_dtypes=(BF16,))
                qkv = _matmul(h, sb_w_in.astype(BF16))
                n_h = d // SB_HEAD_DIM
                o["sb_k"] = _to_heads(qkv, 1, n_seq, tg, (n_h, SB_HEAD_DIM))[:, :tv]
                o["sb_v"] = _to_heads(qkv, 2, n_seq, tg, (n_h, SB_HEAD_DIM))[:, :tv]
                if gi == 0:
                    y = _sb_prompt(qkv, n_seq=n_seq, tg=tg)
                else:
                    y = _sb_sample(qkv, cache_sb_k, cache_sb_v, page_table, t_valid=tv)
                w_out = sb_w_out
            elif kind == 2:
                lam_init = 0.8 - 0.6 * math.exp(-0.3 * i)
                lf = da_lambda.astype(F32)
                lam = (jnp.exp(jnp.sum(lf[0] * lf[1])) - jnp.exp(jnp.sum(lf[2] * lf[3])) + lam_init).reshape(1)
                (h,) = _norm_mod(x, norm_mix[i], m[0], m[1], tg=tg, out_dtypes=(BF16,))
                qkv = _matmul(h, da_w_in.astype(BF16))
                qk = _rope_qk(qkv, pos[gi])
                n_h = d // (2 * DA_HEAD_DIM)
                o["da_k"] = _to_heads(qk, 1, n_seq, tg, (n_h, 2, DA_HEAD_DIM))[:, :tv]
                o["da_v"] = rows_of(qkv[:, 2 * d:], (n_h, 2 * DA_HEAD_DIM))
                if gi == 0:
                    y = _da_prompt(qk, qkv, lam, da_subln, n_seq=n_seq, tg=tg, lam_init=lam_init)
                else:
                    y = _da_sample(qk, qkv, cache_da_k, cache_da_v, page_table, lam, da_subln,
                                   t_valid=tv, lam_init=lam_init)
                w_out = da_w_out
            else:
                (h,) = _norm_mod(x, norm_mix[i], m[0], m[1], tg=tg, out_dtypes=(BF16,))
                proj = _matmul(h, gla_w_in.astype(BF16))
                lo = _matmul(_matmul(h, gla_wa1.astype(BF16), out_dtype=BF16), gla_wa2.astype(BF16))
                y, o["gla"] = _gla(proj, lo, gla_ba, gla_norm, gla_state[gi], n_seq=n_seq, tg=tg, t_valid=tv)
                w_out = gla_w_out
            xs[gi] = _matmul(y, w_out.astype(BF16), resid=x, gate=m[2], tg=tg)

        n_g, n_e = moe_w_group.shape[2], moe_w_expert.shape[3]
        router_w = jnp.concatenate([moe_w_group[i]] + [moe_w_expert[i, g] for g in range(n_g)], axis=1)
        router_w = jnp.pad(router_w, ((0, 0), (0, -router_w.shape[1] % LANES)))
        hs, logits = [], []
        for gi, (n_seq, tg, tv) in enumerate(groups):
            m = mods[gi]
            h, lg = _norm_mod(xs[gi], norm_ffn[i], m[3], m[4], tg=tg, out_dtypes=(F32,), router_w=router_w)
            hs.append(h)
            logits.append(lg)
        xs = _moe_layer(groups, xs, hs, logits, [mods[0][5], mods[1][5]], n_g,
                        moe_w_gate, moe_w_up, moe_w_down, i)

    ys = []
    for gi, (n_seq, tg, tv) in enumerate(groups):
        (y,) = _norm_mod(xs[gi], norm_final, None, None, tg=tg, out_dtypes=(F32,))
        ys.append(y.reshape(n_seq, tg, d)[:, :tv])
    names = ("rw_shift", "rw_wkv", "sb_k", "sb_v", "da_k", "da_v", "gla")
    return (ys[0], ys[1]) + tuple(outs[0][n] for n in names) + tuple(outs[1][n] for n in names)
```

```python
import functools
import math

import jax
import jax.numpy as jnp
from jax import lax
from jax.experimental import pallas as pl
from jax.experimental.pallas import tpu as pltpu

F32 = jnp.float32
BF16 = jnp.bfloat16

LANES = 128
SUBLANES = 8
VMEM_LIMIT_BYTES = 56 * 1024 * 1024

NORM_EPS = 1e-6
ROPE_THETA = 10000.0
RW_HEAD_DIM = 64
RW_GN_EPS = 64e-5
SB_HEAD_DIM = 128
DA_HEAD_DIM = 128
DA_SUBLN_EPS = 1e-5
GLA_HEADS = 4
GLA_TAU = 16.0
GLA_NORM_EPS = 1e-5
MOE_TOPK = 2
SAMPLE_SLOT = 8


def _params(*semantics):
    return pltpu.CompilerParams(dimension_semantics=semantics, vmem_limit_bytes=VMEM_LIMIT_BYTES)


def _split3(x):
    hi = x.astype(BF16)
    r1 = x - hi.astype(F32)
    mid = r1.astype(BF16)
    lo = (r1 - mid.astype(F32)).astype(BF16)
    return hi, mid, lo


_NN = (((1,), (0,)), ((), ()))
_NT = (((1,), (1,)), ((), ()))
_TN = (((0,), (0,)), ((), ()))


def _dot1(a, b, dims=_NN):
    return lax.dot_general(a.astype(BF16), b.astype(BF16), dims, preferred_element_type=F32)


def _dot_exact_rhs(a, b_exact, dims=_NN):
    hi, mid, lo = _split3(a)
    b = b_exact.astype(BF16)
    return (lax.dot_general(hi, b, dims, preferred_element_type=F32)
            + lax.dot_general(mid, b, dims, preferred_element_type=F32)
            + lax.dot_general(lo, b, dims, preferred_element_type=F32))


def _dot_sum16(a, b_exact):
    hi = a.astype(BF16)
    mid = (a - hi.astype(F32)).astype(BF16)
    b = b_exact.astype(BF16)
    return jnp.dot(hi, b, preferred_element_type=F32) + jnp.dot(mid, b, preferred_element_type=F32)


def _dot_exact_lhs(a_exact, b, dims=_NN):
    hi, mid, lo = _split3(b)
    a = a_exact.astype(BF16)
    return (lax.dot_general(a, hi, dims, preferred_element_type=F32)
            + lax.dot_general(a, mid, dims, preferred_element_type=F32)
            + lax.dot_general(a, lo, dims, preferred_element_type=F32))


def _dot3(a, b, dims=_NN):
    ah, am, _ = _split3(a)
    bh, bm, _ = _split3(b)
    d = functools.partial(lax.dot_general, dimension_numbers=dims, preferred_element_type=F32)
    return d(ah, bh) + (d(ah, bm) + d(am, bh))


def _softplus(x):
    return jnp.maximum(x, 0.0) + jnp.log(1.0 + jnp.exp(-jnp.abs(x)))


def _sigmoid(x):
    return 1.0 / (1.0 + jnp.exp(-x))


def _adaln_kernel(c_ref, w_ref, b_ref, o_ref):
    c = c_ref[...]
    s = c * _sigmoid(c)
    o_ref[...] = _dot3(s, w_ref[...]) + b_ref[...]


def _adaln(c_all, ada_w, ada_b):
    n_layers, d, n = ada_w.shape
    rows = c_all.shape[0]
    tn = min(512, n)
    return pl.pallas_call(
        _adaln_kernel,
        out_shape=jax.ShapeDtypeStruct((n_layers, rows, n), F32),
        grid=(n_layers, n // tn),
        in_specs=[
            pl.BlockSpec((rows, d), lambda l, j: (0, 0)),
            pl.BlockSpec((None, d, tn), lambda l, j: (l, 0, j)),
            pl.BlockSpec((None, 1, tn), lambda l, j: (l, 0, j)),
        ],
        out_specs=pl.BlockSpec((None, rows, tn), lambda l, j: (l, 0, j)),
        compiler_params=_params("parallel", "parallel"),
        name="adaln",
    )(c_all, ada_w, ada_b.reshape(n_layers, 1, n))


def _norm_mod_kernel(*refs, eps, modulate, n_out, router):
    x_ref, w_ref = refs[:2]
    pos = 2
    x = x_ref[...]
    y = x * lax.rsqrt(jnp.mean(x * x, axis=-1, keepdims=True) + eps) * w_ref[...]
    if modulate:
        shift_ref, scale_ref = refs[pos:pos + 2]
        pos += 2
        y = y * (1.0 + scale_ref[...]) + shift_ref[...]
    if router:
        wr_ref = refs[pos]
        pos += 1
    outs = refs[pos:]
    for o in outs[:n_out]:
        o[...] = y.astype(o.dtype)
    if router:
        outs[n_out][...] = _dot3(y, wr_ref[...])


def _mod_spec(mod, tm, tg, width):
    g, gr, _ = mod.shape
    if gr == 1:
        per = tg // tm
        return pl.BlockSpec((None, 1, width), lambda i, *_: (i // per, 0, 0))
    assert g == 1 and gr == tm
    return pl.BlockSpec((None, gr, width), lambda i, *_: (0, 0, 0))


def _norm_mod(x, w, shift, scale, *, tg, out_dtypes, eps=NORM_EPS, router_w=None):
    rows, d = x.shape
    tm = min(256, tg) if shift is None or shift.shape[1] == 1 else rows
    in_specs = [pl.BlockSpec((tm, d), lambda i: (i, 0)), pl.BlockSpec((1, d), lambda i: (0, 0))]
    args = [x, w.reshape(1, d)]
    if shift is not None:
        in_specs += [_mod_spec(shift, tm, tg, d), _mod_spec(scale, tm, tg, d)]
        args += [shift, scale]
    out_shape = [jax.ShapeDtypeStruct((rows, d), dt) for dt in out_dtypes]
    out_specs = [pl.BlockSpec((tm, d), lambda i: (i, 0)) for _ in out_dtypes]
    if router_w is not None:
        nr = router_w.shape[1]
        in_specs.append(pl.BlockSpec((d, nr), lambda i: (0, 0)))
        args.append(router_w)
        out_shape.append(jax.ShapeDtypeStruct((rows, nr), F32))
        out_specs.append(pl.BlockSpec((tm, nr), lambda i: (i, 0)))
    return pl.pallas_call(
        functools.partial(_norm_mod_kernel, eps=eps, modulate=shift is not None,
                          n_out=len(out_dtypes), router=router_w is not None),
        out_shape=out_shape,
        grid=(rows // tm,),
        in_specs=in_specs,
        out_specs=out_specs,
        compiler_params=_params("parallel"),
        name="norm_mod",
    )(*args)


def _matmul_kernel(*refs, act, resid):
    a_ref, b_ref = refs[:2]
    o_ref = refs[-1]
    acc = jnp.dot(a_ref[...], b_ref[...], preferred_element_type=F32)
    if act == "tanh":
        acc = jnp.tanh(acc)
    elif act == "sigmoid":
        acc = _sigmoid(acc)
    if resid:
        x_ref, g_ref = refs[2:4]
        acc = x_ref[...] + g_ref[...] * acc
    o_ref[...] = acc.astype(o_ref.dtype)


def _matmul(a, b, *, out_dtype=F32, act=None, resid=None, gate=None, tg=None):
    m, k = a.shape
    n = b.shape[1]
    tm = min(1024, m)
    if gate is not None:
        tm = min(tm, tg) if gate.shape[1] == 1 else m
    tn = min(512, n)
    in_specs = [pl.BlockSpec((tm, k), lambda i, j: (i, 0)), pl.BlockSpec((k, tn), lambda i, j: (0, j))]
    args = [a, b]
    if resid is not None:
        g, gr, _ = gate.shape
        in_specs.append(pl.BlockSpec((tm, tn), lambda i, j: (i, j)))
        if gr == 1:
            per = tg // tm
            in_specs.append(pl.BlockSpec((None, 1, tn), lambda i, j: (i // per, 0, j)))
        else:
            in_specs.append(pl.BlockSpec((None, gr, tn), lambda i, j: (0, 0, j)))
        args += [resid, gate]
    return pl.pallas_call(
        functools.partial(_matmul_kernel, act=act, resid=resid is not None),
        out_shape=jax.ShapeDtypeStruct((m, n), out_dtype),
        grid=(m // tm, n // tn),
        in_specs=in_specs,
        out_specs=pl.BlockSpec((tm, tn), lambda i, j: (i, j)),
        compiler_params=_params("parallel", "parallel"),
        name="matmul",
    )(*args)


def _rw_mix_kernel(h_ref, hp_ref, mu_ref, *o_refs):
    h = h_ref[...]
    xx = hp_ref[...] - h
    for i, o in enumerate(o_refs):
        o[...] = (h + xx * mu_ref[i:i + 1, :]).astype(o.dtype)


def _rw_mix(h, h_prev, mu):
    rows, d = h.shape
    tm = min(256, rows)
    n = mu.shape[0]
    spec = pl.BlockSpec((tm, d), lambda i: (i, 0))
    return pl.pallas_call(
        _rw_mix_kernel,
        out_shape=[jax.ShapeDtypeStruct((rows, d), BF16)] * n,
        grid=(rows // tm,),
        in_specs=[spec, spec, pl.BlockSpec((n, d), lambda i: (0, 0))],
        out_specs=[spec] * n,
        compiler_params=_params("parallel"),
        name="rw_mix",
    )(h, h_prev, mu)


def _rw_recur_kernel(r_ref, k_ref, v_ref, wl_ref, al_ref, g_ref, par_ref, s0_ref,
                     y_ref, sout_ref, s_scr, *, chunk, n_sub, t_valid, precise):
    c = chunk
    n2 = 2 * c
    bi = pl.program_id(2)
    hd = RW_HEAD_DIM

    @pl.when(bi == 0)
    def _():
        s_scr[...] = s0_ref[...]

    par = par_ref[...]
    w0, a0, k_k, k_a, r_k, ln_w, ln_b = (par[i:i + 1, :] for i in range(7))
    lane = lax.broadcasted_iota(jnp.int32, (1, LANES), 1)
    head0 = lane < hd
    lane_r = lax.broadcasted_iota(jnp.int32, (LANES, LANES), 0)
    lane_c = lax.broadcasted_iota(jnp.int32, (LANES, LANES), 1)
    same_head = (lane_r < hd) == (lane_c < hd)
    seg = jnp.where(same_head, 1.0, 0.0)
    ti = lax.broadcasted_iota(jnp.int32, (c, c), 0)
    si = lax.broadcasted_iota(jnp.int32, (c, c), 1)
    tri = jnp.where(ti >= si, 1.0, 0.0)
    rs = lax.broadcasted_iota(jnp.int32, (n2, n2), 0)
    cs = lax.broadcasted_iota(jnp.int32, (n2, n2), 1)
    same_blk = (rs < c) == (cs < c)
    rt = jnp.where(rs < c, rs, rs - c)
    ct = jnp.where(cs < c, cs, cs - c)
    low_s = jnp.logical_and(same_blk, rt > ct)
    low_i = jnp.logical_and(same_blk, rt >= ct)
    eye_n = jnp.where(rs == cs, 1.0, 0.0)
    row_id = lax.broadcasted_iota(jnp.int32, (c, 1), 0)

    def stack(x):
        return jnp.concatenate([x, x], axis=0)

    def stack_heads(x):
        return jnp.concatenate([jnp.where(head0, x, 0.0), jnp.where(head0, 0.0, x)], axis=0)

    def halves(x):
        return jnp.where(head0, x[:c], x[c:])

    subs = range(n_sub)
    rows = [slice(i * c, (i + 1) * c) for i in subs]
    each = lambda f, *xs: [f(*a) for a in zip(*xs)]
    r = [r_ref[rw, :] for rw in rows]
    k = [k_ref[rw, :] for rw in rows]
    v = [v_ref[rw, :] for rw in rows]
    lw = [-jnp.exp(-_softplus(-(w0 + wl_ref[rw, :])) - 0.5) for rw in rows]
    a = [_sigmoid(a0 + al_ref[rw, :]) for rw in rows]
    kk = each(lambda k_: k_ * k_k, k)
    nrm = each(lambda x: jnp.sqrt(_dot_exact_rhs(x * x, seg)), kk)
    kk = each(lambda x, n: x / jnp.maximum(n, 1e-12), kk, nrm)
    kmod = each(lambda k_, a_: k_ * (1.0 + (a_ - 1.0) * k_a), k, a)
    al = each(lambda x: -x, kk)
    be = each(lambda x, a_: x * a_, kk, a)
    kmod_s = kmod
    if t_valid < c * n_sub:
        valid = [(bi * (c * n_sub) + i * c + row_id) < t_valid for i in subs]
        mask = lambda x, m: jnp.where(m, x, 0.0)
        lw, al, be, kmod_s = each(mask, lw, valid), each(mask, al, valid), each(mask, be, valid), each(mask, kmod, valid)

    cum = each(lambda x: _dot_exact_lhs(tri, x), lw)
    p_col = each(lambda x: jnp.exp(jnp.transpose(jnp.broadcast_to(x[c - 1:c, :], (LANES, LANES)))), cum)
    p_i = each(lambda x: jnp.exp(-x), cum)
    p_e = each(lambda x: jnp.exp(x[c - 1:c, :] - x), cum)
    a_t = each(lambda al_, cu, lw_: al_ * jnp.exp(cu - lw_), al, cum, lw)
    b_t = each(lambda x, p: x * p, be, p_i)
    k_t = each(lambda x, p: x * p, kmod_s, p_i)
    r_t = each(lambda x, cu: x * jnp.exp(cu), r, cum)
    b_bar = each(lambda x, p: x * p, be, p_e)
    k_bar = each(lambda x, p: x * p, kmod_s, p_e)

    x_a = each(stack_heads, a_t)
    x_r = each(stack_heads, r_t)
    y_b = each(stack, b_t)
    y_k = each(stack, k_t)
    v_st = each(stack, v)
    mm = _dot3 if precise else _dot1
    l_ab = each(lambda x, y: jnp.where(low_s, mm(x, y, _NT), 0.0), x_a, y_b)
    l_ak = each(lambda x, y: jnp.where(low_s, mm(x, y, _NT), 0.0), x_a, y_k)
    l_rb = each(lambda x, y: jnp.where(low_i, mm(x, y, _NT), 0.0), x_r, y_b)
    l_rk = each(lambda x, y: jnp.where(low_i, mm(x, y, _NT), 0.0), x_r, y_k)
    t_inv = each(lambda l: eye_n + l, l_ab)
    pw = l_ab
    span = 2
    while span < c:
        pw = each(lambda p: mm(p, p), pw)
        t_inv = each(lambda t, p: t + mm(t, p), t_inv, pw)
        span *= 2
    resid = each(lambda t, l: (eye_n - t) + _dot3(l, t), t_inv, l_ab)
    t_inv = each(lambda t, e: t + mm(t, e), t_inv, resid)
    a_hat = each(lambda t, x: halves(mm(t, stack(x))), t_inv, a_t)
    lv = each(mm, l_ak, v_st)
    u_hat = each(lambda t, x: halves(mm(t, x)), t_inv, lv)
    r_hat = each(lambda x, l, ah: x + halves(mm(l, stack(ah))), r_t, l_rb, a_hat)
    y_hat = each(lambda lb, uh, lk, vs: halves(mm(lb, stack(uh)) + mm(lk, vs)), l_rb, u_hat, l_rk, v_st)
    g_low = each(lambda b, ah: jnp.where(same_head, mm(b, ah, _TN), 0.0), b_bar, a_hat)
    z_mat = each(lambda b, uh, kb, v_: jnp.where(same_head, mm(b, uh, _TN) + mm(kb, v_, _TN), 0.0),
                 b_bar, u_hat, k_bar, v)

    s = s_scr[...]
    y = []
    for i in subs:
        y.append(mm(r_hat[i], s) + y_hat[i])
        s = s * p_col[i] + mm(g_low[i], s) + z_mat[i]

    inv_hd = 1.0 / hd
    stat = _dot_exact_rhs if precise else _dot_sum16
    mean = each(lambda x: stat(x, seg) * inv_hd, y)
    yc = each(lambda x, m: x - m, y, mean)
    var = each(lambda x: stat(x * x, seg) * inv_hd, yc)
    bonus = each(lambda r_, km, v_: stat(r_ * km * r_k, seg) * v_, r, kmod, v)
    for i in subs:
        yn = yc[i] * lax.rsqrt(var[i] + RW_GN_EPS) * ln_w + ln_b
        y_ref[rows[i], :] = ((yn + bonus[i]) * g_ref[rows[i], :]).astype(y_ref.dtype)

    s_scr[...] = s

    @pl.when(bi == pl.num_programs(2) - 1)
    def _():
        sout_ref[...] = s


def _rw_recur(r, k, v, wl, al, g, par, s0, *, n_seq, tg, t_valid):
    rows, d = r.shape
    c = min(64, tg)
    n_sub = min(16, tg // c)
    tb = c * n_sub
    n_blocks = tg // tb
    n_pairs = d // LANES
    tile = pl.BlockSpec((tb, LANES), lambda b, p, bi: (b * n_blocks + bi, p))
    state = pl.BlockSpec((None, None, LANES, LANES), lambda b, p, bi: (b, p, 0, 0))
    return pl.pallas_call(
        functools.partial(_rw_recur_kernel, chunk=c, n_sub=n_sub, t_valid=t_valid, precise=tg <= SAMPLE_SLOT),
        out_shape=[jax.ShapeDtypeStruct((rows, d), BF16),
                   jax.ShapeDtypeStruct((n_seq, n_pairs, LANES, LANES), F32)],
        grid=(n_seq, n_pairs, n_blocks),
        in_specs=[tile] * 6 + [pl.BlockSpec((8, LANES), lambda b, p, bi: (0, p)), state],
        out_specs=[tile, state],
        scratch_shapes=[pltpu.VMEM((LANES, LANES), F32)],
        compiler_params=_params("parallel", "parallel", "arbitrary"),
        name="rw_recur",
    )(r, k, v, wl, al, g, par, s0)


def _rw_state_to_pairs(wkv):
    b, h, n, _ = wkv.shape
    st = jnp.swapaxes(wkv, -1, -2).reshape(b, h // 2, 2, n, n)
    z = jnp.zeros_like(st[:, :, 0])
    top = jnp.concatenate([st[:, :, 0], z], axis=-1)
    bot = jnp.concatenate([z, st[:, :, 1]], axis=-1)
    return jnp.concatenate([top, bot], axis=-2)


def _rw_pairs_to_state(s):
    b, p, _, _ = s.shape
    n = RW_HEAD_DIM
    h0 = s[:, :, :n, :n]
    h1 = s[:, :, n:, n:]
    st = jnp.stack([h0, h1], axis=2).reshape(b, 2 * p, n, n)
    return jnp.swapaxes(st, -1, -2)


def _rwkv7(h, shift_prev, wkv_prev, w, *, n_seq, tg, t_valid):
    rows, d = h.shape
    h3 = h.reshape(n_seq, tg, d)
    h_prev = jnp.concatenate([shift_prev[:, None, :], h3[:, :-1]], axis=1).reshape(rows, d)
    xr, xw, xk, xv, xa, xg = _rw_mix(h, h_prev, w["rw_mu"])
    r = _matmul(xr, w["rw_w_r"])
    k = _matmul(xk, w["rw_w_k"])
    v = _matmul(xv, w["rw_w_v"])
    wl = _matmul(_matmul(xw, w["rw_w1"], out_dtype=BF16, act="tanh"), w["rw_w2"])
    al = _matmul(_matmul(xa, w["rw_a1"], out_dtype=BF16), w["rw_a2"])
    g = _matmul(_matmul(xg, w["rw_g1"], out_dtype=BF16, act="sigmoid"), w["rw_g2"])
    par = jnp.stack([w["rw_w0"], w["rw_a0"], w["rw_k_k"], w["rw_k_a"], w["rw_r_k"].reshape(-1),
                     w["rw_ln_w"], w["rw_ln_b"], jnp.zeros((d,), F32)])
    y, s_fin = _rw_recur(r, k, v, wl, al, g, par, _rw_state_to_pairs(wkv_prev),
                         n_seq=n_seq, tg=tg, t_valid=t_valid)
    return y, h3[:, t_valid - 1], _rw_pairs_to_state(s_fin)


def _to_heads_kernel(x_ref, o_ref, *, n_heads, hd):
    tm = x_ref.shape[0]
    for h in range(n_heads):
        o_ref[pl.ds(h, tm, stride=n_heads), :] = x_ref[:, h * hd:(h + 1) * hd]


def _to_heads(x, third, n_seq, tg, head_shape):
    rows = x.shape[0]
    hd = head_shape[-1]
    n_heads = math.prod(head_shape[:-1])
    d = n_heads * hd
    tm = min(256, rows)
    out = pl.pallas_call(
        functools.partial(_to_heads_kernel, n_heads=n_heads, hd=hd),
        out_shape=jax.ShapeDtypeStruct((rows * n_heads, hd), x.dtype),
        grid=(rows // tm,),
        in_specs=[pl.BlockSpec((tm, d), lambda i: (i, third))],
        out_specs=pl.BlockSpec((tm * n_heads, hd), lambda i: (i, 0)),
        compiler_params=_params("parallel"),
        name="to_heads",
    )(x)
    return out.reshape((n_seq, tg) + tuple(head_shape))


def _sb_tile(z, mask, later_carry, u_mat):
    lk0 = -_softplus(z)
    lk = lk0 if mask is None else jnp.where(mask, lk0, 0.0)
    later = _dot_sum16(lk, u_mat)
    w = jnp.exp(z + lk0 + later + later_carry)
    if mask is not None:
        w = jnp.where(mask, w, 0.0)
    return w, later_carry + later[:, 0:1] + lk[:, 0:1]


def _suffix_matrix(n):
    j = lax.broadcasted_iota(jnp.int32, (n, n), 0)
    k = lax.broadcasted_iota(jnp.int32, (n, n), 1)
    return jnp.where(j > k, 1.0, 0.0)


def _sb_prompt_kernel(q_ref, k_ref, v_ref, o_ref, *, bq, n_par):
    qi = pl.program_id(2)
    hd = SB_HEAD_DIM
    scale = hd ** -0.5
    heads = range(n_par)
    cols = [slice(h * hd, (h + 1) * hd) for h in heads]
    q = [q_ref[:, c].astype(BF16) for c in cols]
    u_mat = _suffix_matrix(bq)
    row = lax.broadcasted_iota(jnp.int32, (bq, bq), 0)
    col = lax.broadcasted_iota(jnp.int32, (bq, bq), 1)

    def block(kb, state, mask):
        start = pl.multiple_of(kb * bq, bq)
        z = [lax.dot_general(q[h], k_ref[pl.ds(start, bq), cols[h]].astype(BF16), _NT,
                             preferred_element_type=F32) * scale for h in heads]
        tiles = [_sb_tile(z[h], mask, state[h][1], u_mat) for h in heads]
        return tuple(
            (state[h][0] + jnp.dot(tiles[h][0].astype(BF16), v_ref[pl.ds(start, bq), cols[h]].astype(BF16),
                                   preferred_element_type=F32), tiles[h][1]) for h in heads)

    init = tuple((jnp.zeros((bq, hd), F32), jnp.zeros((bq, 1), F32)) for _ in heads)
    state = block(qi, init, col < row)
    state = lax.fori_loop(1, qi + 1, lambda j, st: block(qi - j, st, None), state)
    for h in heads:
        o_ref[:, cols[h]] = state[h][0].astype(o_ref.dtype)


def _sb_prompt(qkv, *, n_seq, tg):
    rows, d3 = qkv.shape
    d = d3 // 3
    hd = SB_HEAD_DIM
    n_heads = d // hd
    n_par = 2 if n_heads % 2 == 0 else 1
    hw = n_par * hd
    n_hb = n_heads // n_par
    bq = min(256, tg)
    nq = tg // bq
    return pl.pallas_call(
        functools.partial(_sb_prompt_kernel, bq=bq, n_par=n_par),
        out_shape=jax.ShapeDtypeStruct((rows, d), BF16),
        grid=(n_seq, n_hb, nq),
        in_specs=[
            pl.BlockSpec((bq, hw), lambda b, h, qi: (b * nq + qi, h)),
            pl.BlockSpec((tg, hw), lambda b, h, qi: (b, n_hb + h)),
            pl.BlockSpec((tg, hw), lambda b, h, qi: (b, 2 * n_hb + h)),
        ],
        out_specs=pl.BlockSpec((bq, hw), lambda b, h, qi: (b * nq + qi, h)),
        compiler_params=_params("parallel", "parallel", "arbitrary"),
        name="sb_prompt",
    )(qkv, qkv, qkv)


def _sb_sample_kernel(pt_ref, q_ref, kn_ref, vn_ref, kc_ref, vc_ref, o_ref, acc_scr, carry_scr,
                      *, n_heads, t_valid, page):
    j = pl.program_id(1)
    hd = SB_HEAD_DIM
    slot = SAMPLE_SLOT
    scale = hd ** -0.5
    n_rows = n_heads * slot

    def tile(get_k, get_v, nk, first):
        q = q_ref[...].astype(BF16)
        z = jnp.concatenate(
            [lax.dot_general(q[:, h * hd:(h + 1) * hd], get_k(h).astype(BF16), _NT, preferred_element_type=F32)
             for h in range(n_heads)], axis=0) * scale
        if first:
            row = lax.broadcasted_iota(jnp.int32, (n_rows, nk), 0)
            col = lax.broadcasted_iota(jnp.int32, (n_rows, nk), 1)
            mask = col < jnp.minimum(row & (slot - 1), t_valid)
            carry = jnp.zeros((n_rows, 1), F32)
        else:
            mask = None
            carry = carry_scr[:, 0:1]
        w, carry = _sb_tile(z, mask, carry, _suffix_matrix(nk))
        carry_scr[...] = jnp.broadcast_to(carry, (n_rows, LANES))
        wb = w.astype(BF16)
        for h in range(n_heads):
            pv = jnp.dot(wb[h * slot:(h + 1) * slot], get_v(h).astype(BF16), preferred_element_type=F32)
            sl = slice(h * hd, (h + 1) * hd)
            acc_scr[:, sl] = pv if first else acc_scr[:, sl] + pv

    @pl.when(j == 0)
    def _():
        tile(lambda h: kn_ref[:, h * hd:(h + 1) * hd], lambda h: vn_ref[:, h * hd:(h + 1) * hd], slot, True)

    @pl.when(j > 0)
    def _():
        tile(lambda h: kc_ref[pl.ds(h, page, stride=n_heads), :],
             lambda h: vc_ref[pl.ds(h, page, stride=n_heads), :], page, False)

    @pl.when(j == pl.num_programs(1) - 1)
    def _():
        o_ref[...] = acc_scr[...].astype(o_ref.dtype)


def _sb_sample(qkv, cache_k, cache_v, page_table, *, t_valid):
    n_seq, n_pages = page_table.shape
    rows, d3 = qkv.shape
    d = d3 // 3
    n_phys, page, n_heads, hd = cache_k.shape
    slot = SAMPLE_SLOT

    ck = cache_k.reshape(n_phys, page * n_heads, hd)
    cv = cache_v.reshape(n_phys, page * n_heads, hd)
    page_spec = pl.BlockSpec((None, page * n_heads, hd), lambda s, j, pt: (pt[s, n_pages - jnp.maximum(j, 1)], 0, 0))

    grid_spec = pltpu.PrefetchScalarGridSpec(
        num_scalar_prefetch=1,
        grid=(n_seq, n_pages + 1),
        in_specs=[
            pl.BlockSpec((slot, d), lambda s, j, pt: (s, 0)),
            pl.BlockSpec((slot, d), lambda s, j, pt: (s, 1)),
            pl.BlockSpec((slot, d), lambda s, j, pt: (s, 2)),
            page_spec,
            page_spec,
        ],
        out_specs=pl.BlockSpec((slot, d), lambda s, j, pt: (s, 0)),
        scratch_shapes=[pltpu.VMEM((slot, d), F32), pltpu.VMEM((n_heads * slot, LANES), F32)],
    )
    return pl.pallas_call(
        functools.partial(_sb_sample_kernel, n_heads=n_heads, t_valid=t_valid, page=page),
        out_shape=jax.ShapeDtypeStruct((rows, d), BF16),
        grid_spec=grid_spec,
        compiler_params=_params("parallel", "arbitrary"),
        name="sb_sample",
    )(page_table, qkv, qkv, qkv, ck, cv)


def _rope_kernel(x_ref, cs_ref, sn_ref, o_ref):
    cs = cs_ref[...]
    sn = sn_ref[...]
    for c in range(x_ref.shape[1] // LANES):
        x = x_ref[:, c * LANES:(c + 1) * LANES]
        o_ref[:, c * LANES:(c + 1) * LANES] = x * cs + pltpu.roll(x, LANES // 2, 1) * sn


def _rope_tables(pos):
    half = DA_HEAD_DIM // 2
    inv_freq = ROPE_THETA ** (-jnp.arange(half, dtype=F32) / half)
    ang = pos.astype(F32)[:, None] * inv_freq[None, :]
    cos, sin = jnp.cos(ang), jnp.sin(ang)
    return jnp.concatenate([cos, cos], axis=1), jnp.concatenate([-sin, sin], axis=1)


def _rope_qk(qkv, pos_rows):
    rows, d3 = qkv.shape
    d2 = d3 // 3 * 2
    cs, sn = _rope_tables(pos_rows)
    tt = cs.shape[0]
    tm = min(256, tt)
    tn = min(1024, d2)
    per = tt // tm
    return pl.pallas_call(
        _rope_kernel,
        out_shape=jax.ShapeDtypeStruct((rows, d2), F32),
        grid=(rows // tm, d2 // tn),
        in_specs=[
            pl.BlockSpec((tm, tn), lambda i, j: (i, j)),
            pl.BlockSpec((tm, LANES), lambda i, j: (i % per, 0)),
            pl.BlockSpec((tm, LANES), lambda i, j: (i % per, 0)),
        ],
        out_specs=pl.BlockSpec((tm, tn), lambda i, j: (i, j)),
        compiler_params=_params("parallel", "parallel"),
        name="rope",
    )(qkv, cs, sn)


def _softmax_tile(s, mask, m_prev, l_prev):
    if mask is not None:
        s = jnp.where(mask, s, -jnp.inf)
    m_new = jnp.maximum(m_prev, jnp.max(s, axis=-1, keepdims=True))
    p = jnp.exp(s - m_new)
    alpha = jnp.exp(m_prev - m_new)
    return p, m_new, alpha, alpha * l_prev + jnp.sum(p, axis=-1, keepdims=True)


def _da_finish(acc1, l1, acc2, l2, lam, subln, lam_init):
    o = acc1 / l1 - lam * (acc2 / l2)
    o = o * lax.rsqrt(jnp.mean(o * o, axis=-1, keepdims=True) + DA_SUBLN_EPS) * subln
    return o * (1.0 - lam_init)


def _da_prompt_kernel(lam_ref, q_ref, k_ref, v_ref, sub_ref, o_ref, *, bq, lam_init):
    qi = pl.program_id(2)
    hd = DA_HEAD_DIM
    scale = hd ** -0.5
    q1 = q_ref[:, :hd].astype(BF16)
    q2 = q_ref[:, hd:].astype(BF16)
    row = lax.broadcasted_iota(jnp.int32, (bq, bq), 0)
    col = lax.broadcasted_iota(jnp.int32, (bq, bq), 1)

    def block(kb, state, mask):
        m1, l1, acc1, m2, l2, acc2 = state
        start = pl.multiple_of(kb * bq, bq)
        k_blk = k_ref[pl.ds(start, bq), :].astype(BF16)
        v_blk = v_ref[pl.ds(start, bq), :].astype(BF16)
        s1 = lax.dot_general(q1, k_blk[:, :hd], _NT, preferred_element_type=F32) * scale
        s2 = lax.dot_general(q2, k_blk[:, hd:], _NT, preferred_element_type=F32) * scale
        p1, m1, a1, l1 = _softmax_tile(s1, mask, m1, l1)
        p2, m2, a2, l2 = _softmax_tile(s2, mask, m2, l2)
        acc1 = a1 * acc1 + jnp.dot(p1.astype(BF16), v_blk, preferred_element_type=F32)
        acc2 = a2 * acc2 + jnp.dot(p2.astype(BF16), v_blk, preferred_element_type=F32)
        return m1, l1, acc1, m2, l2, acc2

    neg = jnp.full((bq, 1), -jnp.inf, F32)
    zero = jnp.zeros((bq, 1), F32)
    zacc = jnp.zeros((bq, 2 * hd), F32)
    state = lax.fori_loop(0, qi, lambda kb, st: block(kb, st, None), (neg, zero, zacc, neg, zero, zacc))
    _, l1, acc1, _, l2, acc2 = block(qi, state, col <= row)
    o_ref[...] = _da_finish(acc1, l1, acc2, l2, lam_ref[0], sub_ref[...], lam_init).astype(o_ref.dtype)


def _da_prompt(qk, qkv, lam, subln, *, n_seq, tg, lam_init):
    rows, d2 = qk.shape
    d = d2 // 2
    hw = 2 * DA_HEAD_DIM
    n_heads = d // hw
    bq = min(256, tg)
    nq = tg // bq
    return pl.pallas_call(
        functools.partial(_da_prompt_kernel, bq=bq, lam_init=lam_init),
        out_shape=jax.ShapeDtypeStruct((rows, d), BF16),
        grid=(n_seq, n_heads, nq),
        in_specs=[
            pl.BlockSpec(memory_space=pltpu.SMEM),
            pl.BlockSpec((bq, hw), lambda b, h, qi: (b * nq + qi, h)),
            pl.BlockSpec((tg, hw), lambda b, h, qi: (b, n_heads + h)),
            pl.BlockSpec((tg, hw), lambda b, h, qi: (b, 2 * n_heads + h)),
            pl.BlockSpec((1, hw), lambda b, h, qi: (0, 0)),
        ],
        out_specs=pl.BlockSpec((bq, hw), lambda b, h, qi: (b * nq + qi, h)),
        compiler_params=_params("parallel", "parallel", "arbitrary"),
        name="da_prompt",
    )(lam, qk, qk, qkv, subln.reshape(1, hw))


def _da_sample_kernel(pt_ref, q_ref, kn_ref, vn_ref, kc_ref, vlo_ref, vhi_ref, lam_ref, sub_ref, o_ref,
                      m_scr, l_scr, acc_scr, *, n_heads, t_valid, page, lam_init):
    j = pl.program_id(1)
    hd = DA_HEAD_DIM
    hw = 2 * hd
    slot = SAMPLE_SLOT
    scale = hd ** -0.5
    n_rows = 2 * n_heads * slot

    def tile(get_k, get_v, nk, first):
        q = q_ref[...].astype(BF16)
        s = jnp.concatenate(
            [lax.dot_general(q[:, i * hd:(i + 1) * hd], get_k(i // 2, i % 2).astype(BF16), _NT,
                             preferred_element_type=F32) for i in range(2 * n_heads)], axis=0) * scale
        if first:
            row = lax.broadcasted_iota(jnp.int32, (n_rows, nk), 0)
            col = lax.broadcasted_iota(jnp.int32, (n_rows, nk), 1)
            mask = col <= jnp.minimum(row & (slot - 1), t_valid - 1)
            m_prev = jnp.full((n_rows, 1), -jnp.inf, F32)
            l_prev = jnp.zeros((n_rows, 1), F32)
        else:
            mask = None
            m_prev = m_scr[:, 0:1]
            l_prev = l_scr[:, 0:1]
        p, m_new, alpha, l_new = _softmax_tile(s, mask, m_prev, l_prev)
        m_scr[...] = jnp.broadcast_to(m_new, (n_rows, LANES))
        l_scr[...] = jnp.broadcast_to(l_new, (n_rows, LANES))
        pb = p.astype(BF16)
        for h in range(n_heads):
            vb = get_v(h).astype(BF16)
            sl = slice(h * hw, (h + 1) * hw)
            for c in range(2):
                rows = slice((2 * h + c) * slot, (2 * h + c + 1) * slot)
                pv = jnp.dot(pb[rows], vb, preferred_element_type=F32)
                acc_scr[c, :, sl] = pv if first else alpha[rows] * acc_scr[c, :, sl] + pv

    @pl.when(j == 0)
    def _():
        tile(lambda h, c: kn_ref[:, (2 * h + c) * hd:(2 * h + c + 1) * hd],
             lambda h: vn_ref[:, h * hw:(h + 1) * hw], slot, True)

    @pl.when(j > 0)
    def _():
        tile(lambda h, c: kc_ref[pl.ds(2 * h + c, page, stride=2 * n_heads), :],
             lambda h: jnp.concatenate([vlo_ref[pl.ds(h, page, stride=n_heads), :],
                                        vhi_ref[pl.ds(h, page, stride=n_heads), :]], axis=1), page, False)

    @pl.when(j == pl.num_programs(1) - 1)
    def _():
        l_all = l_scr[:, 0:1]
        for h in range(n_heads):
            sl = slice(h * hw, (h + 1) * hw)
            r1 = slice(2 * h * slot, (2 * h + 1) * slot)
            r2 = slice((2 * h + 1) * slot, (2 * h + 2) * slot)
            o = _da_finish(acc_scr[0, :, sl], l_all[r1], acc_scr[1, :, sl], l_all[r2],
                           lam_ref[0], sub_ref[...], lam_init)
            o_ref[:, sl] = o.astype(o_ref.dtype)


def _da_sample(qk, qkv, cache_k, cache_v, page_table, lam, subln, *, t_valid, lam_init):
    rows, d2 = qk.shape
    d = d2 // 2
    hd = DA_HEAD_DIM
    hw = 2 * hd
    n_heads = d // hw
    n_seq, n_pages = page_table.shape
    n_phys, page = cache_k.shape[:2]
    ck = cache_k.reshape(n_phys, page * 2 * n_heads, hd)
    cv = cache_v.reshape(n_phys, page * n_heads, hw)
    slot = SAMPLE_SLOT

    def page_map(half):
        return lambda s, j, pt: (pt[s, n_pages - jnp.maximum(j, 1)], 0, half)

    grid_spec = pltpu.PrefetchScalarGridSpec(
        num_scalar_prefetch=1,
        grid=(n_seq, n_pages + 1),
        in_specs=[
            pl.BlockSpec((slot, d), lambda s, j, pt: (s, 0)),
            pl.BlockSpec((slot, d), lambda s, j, pt: (s, 1)),
            pl.BlockSpec((slot, d), lambda s, j, pt: (s, 2)),
            pl.BlockSpec((None, page * 2 * n_heads, hd), page_map(0)),
            pl.BlockSpec((None, page * n_heads, hd), page_map(0)),
            pl.BlockSpec((None, page * n_heads, hd), page_map(1)),
            pl.BlockSpec(memory_space=pltpu.SMEM),
            pl.BlockSpec((1, hw), lambda s, j, pt: (0, 0)),
        ],
        out_specs=pl.BlockSpec((slot, d), lambda s, j, pt: (s, 0)),
        scratch_shapes=[pltpu.VMEM((2 * n_heads * slot, LANES), F32),
                        pltpu.VMEM((2 * n_heads * slot, LANES), F32),
                        pltpu.VMEM((2, slot, d), F32)],
    )
    return pl.pallas_call(
        functools.partial(_da_sample_kernel, n_heads=n_heads, t_valid=t_valid, page=page, lam_init=lam_init),
        out_shape=jax.ShapeDtypeStruct((rows, d), BF16),
        grid_spec=grid_spec,
        compiler_params=_params("parallel", "arbitrary"),
        name="da_sample",
    )(page_table, qk, qk, qkv, ck, cv, cv, lam, subln.reshape(1, hw))


def _gla_kernel(q_ref, k_ref, v_ref, gate_ref, lo_ref, ba_ref, nw_ref, s0_ref, o_ref, sout_ref, s_scr,
                *, tb, c, tg, t_valid, dk):
    ci = pl.program_id(2)

    @pl.when(ci == 0)
    def _():
        s_scr[...] = s0_ref[...]

    ti = lax.broadcasted_iota(jnp.int32, (c, c), 0)
    si = lax.broadcasted_iota(jnp.int32, (c, c), 1)
    tri = jnp.where(ti >= si, 1.0, 0.0)
    row_id = lax.broadcasted_iota(jnp.int32, (c, 1), 0)
    ones_cl = jnp.ones((c, LANES), F32)
    ba = ba_ref[...]
    nw = nw_ref[...]
    for sub in range(tb // c):
        rows = slice(sub * c, (sub + 1) * c)
        q = q_ref[rows, :] * (dk ** -0.5)
        k = k_ref[rows, :]
        v = v_ref[rows, :]
        la = -_softplus(-(lo_ref[rows, :] + ba)) / GLA_TAU
        if t_valid < tg:
            valid = (ci * tb + sub * c + row_id) < t_valid
            la = jnp.where(valid, la, 0.0)
            k = jnp.where(valid, k, 0.0)
        cum = _dot_exact_lhs(tri, la)
        cum_end = cum[c - 1:c, :]
        s = s_scr[...]
        o = _dot1(q * jnp.exp(cum), s)
        att = jnp.zeros((c, c), F32)
        for j in range(c):
            e = jnp.exp(jnp.where(row_id >= j, cum - cum[j:j + 1, :], -jnp.inf))
            p = jnp.sum(q * k[j:j + 1, :] * e, axis=-1, keepdims=True)
            att = jnp.where(si == j, p, att)
        o = o + _dot1(att, v)
        dec = jnp.exp(_dot_exact_rhs(la, ones_cl, _TN)[:, 0:1])
        s_scr[...] = s * dec + _dot1(k * jnp.exp(cum_end - cum), v, _TN)
        on = o * lax.rsqrt(jnp.mean(o * o, axis=-1, keepdims=True) + GLA_NORM_EPS) * nw
        g = gate_ref[rows, :]
        o_ref[rows, :] = (on * (g * _sigmoid(g))).astype(o_ref.dtype)

    @pl.when(ci == pl.num_programs(2) - 1)
    def _():
        sout_ref[...] = s_scr[...]


def _gla(proj, lo, ba, norm_w, s0, *, n_seq, tg, t_valid):
    rows = proj.shape[0]
    key = lo.shape[1]
    d = (proj.shape[1] - 2 * key) // 2
    nh = GLA_HEADS
    dk, dv = key // nh, d // nh
    tb = min(128, tg)
    c = min(32, tb)
    nc = tg // tb
    kv = key // dv
    return pl.pallas_call(
        functools.partial(_gla_kernel, tb=tb, c=c, tg=tg, t_valid=t_valid, dk=dk),
        out_shape=[jax.ShapeDtypeStruct((rows, d), BF16), jax.ShapeDtypeStruct((n_seq, nh, dk, dv), F32)],
        grid=(n_seq, nh, nc),
        in_specs=[
            pl.BlockSpec((tb, dk), lambda b, h, ci: (b * nc + ci, h)),
            pl.BlockSpec((tb, dk), lambda b, h, ci: (b * nc + ci, nh + h)),
            pl.BlockSpec((tb, dv), lambda b, h, ci: (b * nc + ci, 2 * kv + h)),
            pl.BlockSpec((tb, dv), lambda b, h, ci: (b * nc + ci, 2 * kv + nh + h)),
            pl.BlockSpec((tb, dk), lambda b, h, ci: (b * nc + ci, h)),
            pl.BlockSpec((1, dk), lambda b, h, ci: (0, h)),
            pl.BlockSpec((1, dv), lambda b, h, ci: (0, 0)),
            pl.BlockSpec((None, None, dk, dv), lambda b, h, ci: (b, h, 0, 0)),
        ],
        out_specs=[pl.BlockSpec((tb, dv), lambda b, h, ci: (b * nc + ci, h)),
                   pl.BlockSpec((None, None, dk, dv), lambda b, h, ci: (b, h, 0, 0))],
        scratch_shapes=[pltpu.VMEM((dk, dv), F32)],
        compiler_params=_params("parallel", "parallel", "arbitrary"),
        name="gla",
    )(proj, proj, proj, proj, lo, ba.reshape(1, key), norm_w.reshape(1, dv), s0)


MOE_BLOCK_ROWS = 128


def _moe_route(logits, n_groups, n_experts):
    g_logits = logits[:, :n_groups]
    grp = jnp.argmax(g_logits, axis=-1)
    g_w = jnp.take_along_axis(jax.nn.softmax(g_logits, axis=-1), grp[:, None], axis=-1)
    e_logits = logits[:, n_groups:n_groups + n_groups * n_experts].reshape(-1, n_groups, n_experts)
    e_logits = jnp.take_along_axis(e_logits, grp[:, None, None], axis=1)[:, 0]
    top_val, top_idx = lax.top_k(e_logits, MOE_TOPK)
    gate = g_w * jax.nn.softmax(top_val, axis=-1)
    return (grp[:, None] * n_experts + top_idx).astype(jnp.int32), gate


def _moe_plan(expert_id, n_exp):
    tm = MOE_BLOCK_ROWS
    n_tok, top_k = expert_id.shape
    n_assign = n_tok * top_k
    e_flat = expert_id.reshape(-1)
    order = jnp.argsort(e_flat)
    inv = jnp.argsort(order)
    counts = jnp.bincount(e_flat, length=n_exp)
    start = jnp.cumsum(counts) - counts
    padded = (counts + tm - 1) // tm * tm
    pad_end = jnp.cumsum(padded)
    pad_start = pad_end - padded
    pos = (inv + (pad_start - start)[e_flat]).astype(jnp.int32).reshape(n_tok, top_k)
    n_blocks = -(-n_assign // tm) + n_exp
    block_expert = jnp.minimum(jnp.searchsorted(pad_end, jnp.arange(n_blocks) * tm, side="right"), n_exp - 1)
    row = jnp.arange(n_blocks * tm)
    e_row = jnp.repeat(block_expert, tm)
    rank = row - pad_start[e_row]
    src = order[jnp.clip(rank + start[e_row], 0, n_assign - 1)] // top_k
    src_tok = jnp.where(rank < counts[e_row], src, 0).astype(jnp.int32)
    n_used = (pad_end[-1] // tm).astype(jnp.int32).reshape(1)
    return src_tok, pos, block_expert.astype(jnp.int32), n_used


def _new_expert(be_ref, b):
    return jnp.logical_or(b == 0, be_ref[b] != be_ref[jnp.maximum(b - 1, 0)])


def _moe_up_kernel(be_ref, nu_ref, x_ref, wg_ref, wu_ref, h_ref, wg_bf, wu_bf):
    b = pl.program_id(0)
    used = b < nu_ref[0]

    @pl.when(jnp.logical_and(used, _new_expert(be_ref, b)))
    def _():
        wg_bf[...] = wg_ref[...].astype(BF16)
        wu_bf[...] = wu_ref[...].astype(BF16)

    @pl.when(used)
    def _():
        x = x_ref[...].astype(BF16)
        g = jnp.dot(x, wg_bf[...], preferred_element_type=F32)
        u = jnp.dot(x, wu_bf[...], preferred_element_type=F32)
        h_ref[...] = (g * _sigmoid(g) * u).astype(h_ref.dtype)

    @pl.when(jnp.logical_not(used))
    def _():
        h_ref[...] = jnp.zeros_like(h_ref)


def _moe_down_kernel(be_ref, nu_ref, h_ref, wd_ref, o_ref, wd_bf):
    b = pl.program_id(0)
    used = b < nu_ref[0]

    @pl.when(jnp.logical_and(used, _new_expert(be_ref, b)))
    def _():
        wd_bf[...] = wd_ref[...].astype(BF16)

    @pl.when(used)
    def _():
        o_ref[...] = jnp.dot(h_ref[...], wd_bf[...], preferred_element_type=F32)

    @pl.when(jnp.logical_not(used))
    def _():
        o_ref[...] = jnp.zeros_like(o_ref)


def _moe_experts(rows, block_expert, n_used, w_gate, w_up, w_down, layer):
    tm = MOE_BLOCK_ROWS
    n_rows, d = rows.shape
    f = w_gate.shape[3]
    n_blocks = n_rows // tm
    up_spec = pltpu.PrefetchScalarGridSpec(
        num_scalar_prefetch=2,
        grid=(n_blocks,),
        in_specs=[
            pl.BlockSpec((tm, d), lambda b, be, nu: (b, 0)),
            pl.BlockSpec((None, None, d, f), lambda b, be, nu: (layer, be[b], 0, 0)),
            pl.BlockSpec((None, None, d, f), lambda b, be, nu: (layer, be[b], 0, 0)),
        ],
        out_specs=pl.BlockSpec((tm, f), lambda b, be, nu: (b, 0)),
        scratch_shapes=[pltpu.VMEM((d, f), BF16), pltpu.VMEM((d, f), BF16)],
    )
    h = pl.pallas_call(
        _moe_up_kernel,
        out_shape=jax.ShapeDtypeStruct((n_rows, f), BF16),
        grid_spec=up_spec,
        compiler_params=_params("arbitrary"),
        name="moe_up",
    )(block_expert, n_used, rows, w_gate, w_up)
    down_spec = pltpu.PrefetchScalarGridSpec(
        num_scalar_prefetch=2,
        grid=(n_blocks,),
        in_specs=[
            pl.BlockSpec((tm, f), lambda b, be, nu: (b, 0)),
            pl.BlockSpec((None, None, f, d), lambda b, be, nu: (layer, be[b], 0, 0)),
        ],
        out_specs=pl.BlockSpec((tm, d), lambda b, be, nu: (b, 0)),
        scratch_shapes=[pltpu.VMEM((f, d), BF16)],
    )
    return pl.pallas_call(
        _moe_down_kernel,
        out_shape=jax.ShapeDtypeStruct((n_rows, d), F32),
        grid_spec=down_spec,
        compiler_params=_params("arbitrary"),
        name="moe_down",
    )(block_expert, n_used, h, w_down)


def _moe_combine_kernel(x_ref, o0_ref, o1_ref, g_ref, m_ref, y_ref):
    g = g_ref[...]
    y = o0_ref[...] * g[:, 0:1] + o1_ref[...] * g[:, 1:2]
    y_ref[...] = x_ref[...] + m_ref[...] * y


def _moe_combine(x, o0, o1, gates, gmod, *, tg):
    rows, d = x.shape
    tm = min(256, tg) if gmod.shape[1] == 1 else rows
    tile = pl.BlockSpec((tm, d), lambda i: (i, 0))
    return pl.pallas_call(
        _moe_combine_kernel,
        out_shape=jax.ShapeDtypeStruct((rows, d), F32),
        grid=(rows // tm,),
        in_specs=[tile, tile, tile, pl.BlockSpec((tm, MOE_TOPK), lambda i: (i, 0)), _mod_spec(gmod, tm, tg, d)],
        out_specs=tile,
        compiler_params=_params("parallel"),
        name="moe_combine",
    )(x, o0, o1, gates, gmod)


def _moe_layer(groups, xs, hs, logits, gmods, n_groups, w_gate, w_up, w_down, layer):
    n_experts = w_gate.shape[1] // n_groups
    d = xs[0].shape[1]
    real_h, real_l = [], []
    for grp, h, lg in zip(groups, hs, logits):
        n_seq, tg, tv = grp
        real_h.append(h.reshape(n_seq, tg, d)[:, :tv].reshape(n_seq * tv, d))
        real_l.append(lg.reshape(n_seq, tg, -1)[:, :tv].reshape(n_seq * tv, -1))
    h_all = jnp.concatenate(real_h, axis=0)
    expert_id, gate = _moe_route(jnp.concatenate(real_l, axis=0), n_groups, n_experts)
    src_tok, pos, block_expert, n_used = _moe_plan(expert_id, n_groups * n_experts)
    out = _moe_experts(h_all[src_tok], block_expert, n_used, w_gate, w_up, w_down, layer)
    new_xs = []
    start = 0
    for grp, x, gmod in zip(groups, xs, gmods):
        n_seq, tg, tv = grp
        n = n_seq * tv
        rows = pos[start:start + n]
        parts = []
        for a in (out[rows[:, 0]], out[rows[:, 1]], gate[start:start + n]):
            a = a.reshape(n_seq, tv, -1)
            if tv < tg:
                a = jnp.pad(a, ((0, 0), (0, tg - tv), (0, 0)))
            parts.append(a.reshape(n_seq * tg, -1))
        new_xs.append(_moe_combine(x, parts[0], parts[1], parts[2], gmod, tg=tg))
        start += n
    return new_xs


def kernel(x_prompt, x_sample, c_prompt, c_sample, state_rwkv_shift, state_rwkv_wkv, cache_sb_k, cache_sb_v, cache_da_k, cache_da_v, state_gla, page_table, ada_w, ada_b, norm_mix, norm_ffn, norm_final, rw_mu, rw_w_r, rw_w_k, rw_w_v, rw_w_o, rw_w0, rw_w1, rw_w2, rw_a0, rw_a1, rw_a2, rw_g1, rw_g2, rw_k_k, rw_k_a, rw_r_k, rw_ln_w, rw_ln_b, sb_w_in, sb_w_out, da_w_in, da_w_out, da_lambda, da_subln, gla_w_in, gla_wa1, gla_wa2, gla_ba, gla_norm, gla_w_out, moe_w_group, moe_w_expert, moe_w_gate, moe_w_up, moe_w_down):
    n_p, t_p, d = x_prompt.shape
    n_s, t_s, _ = x_sample.shape
    depth = ada_w.shape[0]
    slot = SAMPLE_SLOT
    n_pages, page = page_table.shape[1], cache_sb_k.shape[1]
    past_len = n_pages * page
    groups = [(n_p, t_p, t_p), (n_s, slot, t_s)]

    n_c = n_p + n_s
    c_all = jnp.concatenate([c_prompt, c_sample], axis=0)
    c_all = jnp.pad(c_all, ((0, -n_c % SUBLANES), (0, 0)))
    mod = _adaln(c_all, ada_w, ada_b)

    xs = [x_prompt.reshape(n_p * t_p, d),
          jnp.pad(x_sample, ((0, 0), (0, slot - t_s), (0, 0))).reshape(n_s * slot, d)]
    rw = dict(rw_mu=rw_mu, rw_w0=rw_w0, rw_a0=rw_a0, rw_k_k=rw_k_k, rw_k_a=rw_k_a, rw_r_k=rw_r_k,
              rw_ln_w=rw_ln_w, rw_ln_b=rw_ln_b)
    for name, val in (("rw_w_r", rw_w_r), ("rw_w_k", rw_w_k), ("rw_w_v", rw_w_v), ("rw_w1", rw_w1),
                      ("rw_w2", rw_w2), ("rw_a1", rw_a1), ("rw_a2", rw_a2), ("rw_g1", rw_g1), ("rw_g2", rw_g2)):
        rw[name] = val.astype(BF16)
    pos = [jnp.arange(t_p, dtype=jnp.int32), jnp.tile(past_len + jnp.arange(slot, dtype=jnp.int32), n_s)]
    rw_state = [(jnp.zeros((n_p, d), F32), jnp.zeros((n_p, d // RW_HEAD_DIM, RW_HEAD_DIM, RW_HEAD_DIM), F32)),
                (state_rwkv_shift, state_rwkv_wkv)]
    gla_state = [jnp.zeros((n_p,) + state_gla.shape[1:], F32), state_gla]
    outs = [dict(), dict()]

    for i in range(depth):
        mods = []
        for gi, (n_seq, tg, tv) in enumerate(groups):
            rows = mod[i, :n_p] if gi == 0 else mod[i, n_p:n_c]
            six = [rows[:, j * d:(j + 1) * d] for j in range(6)]
            if gi == 0:
                mods.append([m[:, None, :] for m in six])
            else:
                mods.append([jnp.repeat(m, tg, axis=0)[None] for m in six])
        kind = i % 4
        for gi, (n_seq, tg, tv) in enumerate(groups):
            m = mods[gi]
            x = xs[gi]
            o = outs[gi]

            def rows_of(a, width_shape):
                return a.reshape((n_seq, tg) + width_shape)[:, :tv]

            if kind == 0:
                (h,) = _norm_mod(x, norm_mix[i], m[0], m[1], tg=tg, out_dtypes=(F32,))
                y, o["rw_shift"], o["rw_wkv"] = _rwkv7(h, rw_state[gi][0], rw_state[gi][1], rw,
                                                       n_seq=n_seq, tg=tg, t_valid=tv)
                w_out = rw_w_o
            elif kind == 1:
                (h,) = _norm_mod(x, norm_mix[i], m[0], m[1], tg=tg, out_dtypes=(BF16,))
                qkv = _matmul(h, sb_w_in.astype(BF16))
                n_h = d // SB_HEAD_DIM
                o["sb_k"] = _to_heads(qkv, 1, n_seq, tg, (n_h, SB_HEAD_DIM))[:, :tv]
                o["sb_v"] = _to_heads(qkv, 2, n_seq, tg, (n_h, SB_HEAD_DIM))[:, :tv]
                if gi == 0:
                    y = _sb_prompt(qkv, n_seq=n_seq, tg=tg)
                else:
                    y = _sb_sample(qkv, cache_sb_k, cache_sb_v, page_table, t_valid=tv)
                w_out = sb_w_out
            elif kind == 2:
                lam_init = 0.8 - 0.6 * math.exp(-0.3 * i)
                lf = da_lambda.astype(F32)
                lam = (jnp.exp(jnp.sum(lf[0] * lf[1])) - jnp.exp(jnp.sum(lf[2] * lf[3])) + lam_init).reshape(1)
                (h,) = _norm_mod(x, norm_mix[i], m[0], m[1], tg=tg, out_dtypes=(BF16,))
                qkv = _matmul(h, da_w_in.astype(BF16))
                qk = _rope_qk(qkv, pos[gi])
                n_h = d // (2 * DA_HEAD_DIM)
                o["da_k"] = _to_heads(qk, 1, n_seq, tg, (n_h, 2, DA_HEAD_DIM))[:, :tv]
                o["da_v"] = rows_of(qkv[:, 2 * d:], (n_h, 2 * DA_HEAD_DIM))
                if gi == 0:
                    y = _da_prompt(qk, qkv, lam, da_subln, n_seq=n_seq, tg=tg, lam_init=lam_init)
                else:
                    y = _da_sample(qk, qkv, cache_da_k, cache_da_v, page_table, lam, da_subln,
                                   t_valid=tv, lam_init=lam_init)
                w_out = da_w_out
            else:
                (h,) = _norm_mod(x, norm_mix[i], m[0], m[1], tg=tg, out_dtypes=(BF16,))
                proj = _matmul(h, gla_w_in.astype(BF16))
                lo = _matmul(_matmul(h, gla_wa1.astype(BF16), out_dtype=BF16), gla_wa2.astype(BF16))
                y, o["gla"] = _gla(proj, lo, gla_ba, gla_norm, gla_state[gi], n_seq=n_seq, tg=tg, t_valid=tv)
                w_out = gla_w_out
            xs[gi] = _matmul(y, w_out.astype(BF16), resid=x, gate=m[2], tg=tg)

        n_g, n_e = moe_w_group.shape[2], moe_w_expert.shape[3]
        router_w = jnp.concatenate([moe_w_group[i]] + [moe_w_expert[i, g] for g in range(n_g)], axis=1)
        router_w = jnp.pad(router_w, ((0, 0), (0, -router_w.shape[1] % LANES)))
        hs, logits = [], []
        for gi, (n_seq, tg, tv) in enumerate(groups):
            m = mods[gi]
            h, lg = _norm_mod(xs[gi], norm_ffn[i], m[3], m[4], tg=tg, out_dtypes=(F32,), router_w=router_w)
            hs.append(h)
            logits.append(lg)
        xs = _moe_layer(groups, xs, hs, logits, [mods[0][5], mods[1][5]], n_g,
                        moe_w_gate, moe_w_up, moe_w_down, i)

    ys = []
    for gi, (n_seq, tg, tv) in enumerate(groups):
        (y,) = _norm_mod(xs[gi], norm_final, None, None, tg=tg, out_dtypes=(F32,))
        ys.append(y.reshape(n_seq, tg, d)[:, :tv])
    names = ("rw_shift", "rw_wkv", "sb_k", "sb_v", "da_k", "da_v", "gla")
    return (ys[0], ys[1]) + tuple(outs[0][n] for n in names) + tuple(outs[1][n] for n in names)
```

```python
import functools
import math

import jax
import jax.numpy as jnp
from jax import lax
from jax.experimental import pallas as pl
from jax.experimental.pallas import tpu as pltpu

F32 = jnp.float32
BF16 = jnp.bfloat16

LANES = 128
SUBLANES = 8
VMEM_LIMIT_BYTES = 56 * 1024 * 1024

NORM_EPS = 1e-6
ROPE_THETA = 10000.0
RW_HEAD_DIM = 64
RW_GN_EPS = 64e-5
SB_HEAD_DIM = 128
DA_HEAD_DIM = 128
DA_SUBLN_EPS = 1e-5
GLA_HEADS = 4
GLA_TAU = 16.0
GLA_NORM_EPS = 1e-5
MOE_TOPK = 2
SAMPLE_SLOT = 8


def _params(*semantics):
    return pltpu.CompilerParams(dimension_semantics=semantics, vmem_limit_bytes=VMEM_LIMIT_BYTES)


def _split3(x):
    hi = x.astype(BF16)
    r1 = x - hi.astype(F32)
    mid = r1.astype(BF16)
    lo = (r1 - mid.astype(F32)).astype(BF16)
    return hi, mid, lo


_NN = (((1,), (0,)), ((), ()))
_NT = (((1,), (1,)), ((), ()))
_TN = (((0,), (0,)), ((), ()))


def _dot1(a, b, dims=_NN):
    return lax.dot_general(a.astype(BF16), b.astype(BF16), dims, preferred_element_type=F32)


def _dot_exact_rhs(a, b_exact, dims=_NN):
    hi, mid, lo = _split3(a)
    b = b_exact.astype(BF16)
    return (lax.dot_general(hi, b, dims, preferred_element_type=F32)
            + lax.dot_general(mid, b, dims, preferred_element_type=F32)
            + lax.dot_general(lo, b, dims, preferred_element_type=F32))


def _dot_sum16(a, b_exact):
    hi = a.astype(BF16)
    mid = (a - hi.astype(F32)).astype(BF16)
    b = b_exact.astype(BF16)
    return jnp.dot(hi, b, preferred_element_type=F32) + jnp.dot(mid, b, preferred_element_type=F32)


def _dot_exact_lhs(a_exact, b, dims=_NN):
    hi, mid, lo = _split3(b)
    a = a_exact.astype(BF16)
    return (lax.dot_general(a, hi, dims, preferred_element_type=F32)
            + lax.dot_general(a, mid, dims, preferred_element_type=F32)
            + lax.dot_general(a, lo, dims, preferred_element_type=F32))


def _dot3(a, b, dims=_NN):
    ah, am, _ = _split3(a)
    bh, bm, _ = _split3(b)
    d = functools.partial(lax.dot_general, dimension_numbers=dims, preferred_element_type=F32)
    return d(ah, bh) + (d(ah, bm) + d(am, bh))


def _softplus(x):
    return jnp.maximum(x, 0.0) + jnp.log(1.0 + jnp.exp(-jnp.abs(x)))


def _sigmoid(x):
    return 1.0 / (1.0 + jnp.exp(-x))


def _adaln_kernel(c_ref, w_ref, b_ref, o_ref):
    c = c_ref[...]
    s = c * _sigmoid(c)
    o_ref[...] = _dot3(s, w_ref[...]) + b_ref[...]


def _adaln(c_all, ada_w, ada_b):
    n_layers, d, n = ada_w.shape
    rows = c_all.shape[0]
    tn = min(512, n)
    return pl.pallas_call(
        _adaln_kernel,
        out_shape=jax.ShapeDtypeStruct((n_layers, rows, n), F32),
        grid=(n_layers, n // tn),
        in_specs=[
            pl.BlockSpec((rows, d), lambda l, j: (0, 0)),
            pl.BlockSpec((None, d, tn), lambda l, j: (l, 0, j)),
            pl.BlockSpec((None, 1, tn), lambda l, j: (l, 0, j)),
        ],
        out_specs=pl.BlockSpec((None, rows, tn), lambda l, j: (l, 0, j)),
        compiler_params=_params("parallel", "parallel"),
        name="adaln",
    )(c_all, ada_w, ada_b.reshape(n_layers, 1, n))


def _norm_mod_kernel(*refs, eps, modulate, n_out, router):
    x_ref, w_ref = refs[:2]
    pos = 2
    x = x_ref[...]
    y = x * lax.rsqrt(jnp.mean(x * x, axis=-1, keepdims=True) + eps) * w_ref[...]
    if modulate:
        shift_ref, scale_ref = refs[pos:pos + 2]
        pos += 2
        y = y * (1.0 + scale_ref[...]) + shift_ref[...]
    if router:
        wr_ref = refs[pos]
        pos += 1
    outs = refs[pos:]
    for o in outs[:n_out]:
        o[...] = y.astype(o.dtype)
    if router:
        outs[n_out][...] = _dot3(y, wr_ref[...])


def _mod_spec(mod, tm, tg, width):
    g, gr, _ = mod.shape
    if gr == 1:
        per = tg // tm
        return pl.BlockSpec((None, 1, width), lambda i, *_: (i // per, 0, 0))
    assert g == 1 and gr == tm
    return pl.BlockSpec((None, gr, width), lambda i, *_: (0, 0, 0))


def _norm_mod(x, w, shift, scale, *, tg, out_dtypes, eps=NORM_EPS, router_w=None):
    rows, d = x.shape
    tm = min(256, tg) if shift is None or shift.shape[1] == 1 else rows
    in_specs = [pl.BlockSpec((tm, d), lambda i: (i, 0)), pl.BlockSpec((1, d), lambda i: (0, 0))]
    args = [x, w.reshape(1, d)]
    if shift is not None:
        in_specs += [_mod_spec(shift, tm, tg, d), _mod_spec(scale, tm, tg, d)]
        args += [shift, scale]
    out_shape = [jax.ShapeDtypeStruct((rows, d), dt) for dt in out_dtypes]
    out_specs = [pl.BlockSpec((tm, d), lambda i: (i, 0)) for _ in out_dtypes]
    if router_w is not None:
        nr = router_w.shape[1]
        in_specs.append(pl.BlockSpec((d, nr), lambda i: (0, 0)))
        args.append(router_w)
        out_shape.append(jax.ShapeDtypeStruct((rows, nr), F32))
        out_specs.append(pl.BlockSpec((tm, nr), lambda i: (i, 0)))
    return pl.pallas_call(
        functools.partial(_norm_mod_kernel, eps=eps, modulate=shift is not None,
                          n_out=len(out_dtypes), router=router_w is not None),
        out_shape=out_shape,
        grid=(rows // tm,),
        in_specs=in_specs,
        out_specs=out_specs,
        compiler_params=_params("parallel"),
        name="norm_mod",
    )(*args)


def _matmul_kernel(*refs, act, resid):
    a_ref, b_ref = refs[:2]
    o_ref = refs[-1]
    acc = jnp.dot(a_ref[...], b_ref[...], preferred_element_type=F32)
    if act == "tanh":
        acc = jnp.tanh(acc)
    elif act == "sigmoid":
        acc = _sigmoid(acc)
    if resid:
        x_ref, g_ref = refs[2:4]
        acc = x_ref[...] + g_ref[...] * acc
    o_ref[...] = acc.astype(o_ref.dtype)


def _matmul(a, b, *, out_dtype=F32, act=None, resid=None, gate=None, tg=None):
    m, k = a.shape
    n = b.shape[1]
    tm = min(1024, m)
    if gate is not None:
        tm = min(tm, tg) if gate.shape[1] == 1 else m
    tn = min(512, n)
    in_specs = [pl.BlockSpec((tm, k), lambda i, j: (i, 0)), pl.BlockSpec((k, tn), lambda i, j: (0, j))]
    args = [a, b]
    if resid is not None:
        g, gr, _ = gate.shape
        in_specs.append(pl.BlockSpec((tm, tn), lambda i, j: (i, j)))
        if gr == 1:
            per = tg // tm
            in_specs.append(pl.BlockSpec((None, 1, tn), lambda i, j: (i // per, 0, j)))
        else:
            in_specs.append(pl.BlockSpec((None, gr, tn), lambda i, j: (0, 0, j)))
        args += [resid, gate]
    return pl.pallas_call(
        functools.partial(_matmul_kernel, act=act, resid=resid is not None),
        out_shape=jax.ShapeDtypeStruct((m, n), out_dtype),
        grid=(m // tm, n // tn),
        in_specs=in_specs,
        out_specs=pl.BlockSpec((tm, tn), lambda i, j: (i, j)),
        compiler_params=_params("parallel", "parallel"),
        name="matmul",
    )(*args)


def _rw_mix_kernel(h_ref, hp_ref, mu_ref, *o_refs):
    h = h_ref[...]
    xx = hp_ref[...] - h
    for i, o in enumerate(o_refs):
        o[...] = (h + xx * mu_ref[i:i + 1, :]).astype(o.dtype)


def _rw_mix(h, h_prev, mu):
    rows, d = h.shape
    tm = min(256, rows)
    n = mu.shape[0]
    spec = pl.BlockSpec((tm, d), lambda i: (i, 0))
    return pl.pallas_call(
        _rw_mix_kernel,
        out_shape=[jax.ShapeDtypeStruct((rows, d), BF16)] * n,
        grid=(rows // tm,),
        in_specs=[spec, spec, pl.BlockSpec((n, d), lambda i: (0, 0))],
        out_specs=[spec] * n,
        compiler_params=_params("parallel"),
        name="rw_mix",
    )(h, h_prev, mu)


def _rw_recur_kernel(r_ref, k_ref, v_ref, wl_ref, al_ref, g_ref, par_ref, s0_ref,
                     y_ref, sout_ref, s_scr, *, chunk, n_sub, t_valid, precise):
    c = chunk
    n2 = 2 * c
    bi = pl.program_id(2)
    hd = RW_HEAD_DIM

    @pl.when(bi == 0)
    def _():
        s_scr[...] = s0_ref[...]

    par = par_ref[...]
    w0, a0, k_k, k_a, r_k, ln_w, ln_b = (par[i:i + 1, :] for i in range(7))
    lane = lax.broadcasted_iota(jnp.int32, (1, LANES), 1)
    head0 = lane < hd
    lane_r = lax.broadcasted_iota(jnp.int32, (LANES, LANES), 0)
    lane_c = lax.broadcasted_iota(jnp.int32, (LANES, LANES), 1)
    same_head = (lane_r < hd) == (lane_c < hd)
    seg = jnp.where(same_head, 1.0, 0.0)
    ti = lax.broadcasted_iota(jnp.int32, (c, c), 0)
    si = lax.broadcasted_iota(jnp.int32, (c, c), 1)
    tri = jnp.where(ti >= si, 1.0, 0.0)
    rs = lax.broadcasted_iota(jnp.int32, (n2, n2), 0)
    cs = lax.broadcasted_iota(jnp.int32, (n2, n2), 1)
    same_blk = (rs < c) == (cs < c)
    rt = jnp.where(rs < c, rs, rs - c)
    ct = jnp.where(cs < c, cs, cs - c)
    low_s = jnp.logical_and(same_blk, rt > ct)
    low_i = jnp.logical_and(same_blk, rt >= ct)
    eye_n = jnp.where(rs == cs, 1.0, 0.0)
    row_id = lax.broadcasted_iota(jnp.int32, (c, 1), 0)

    def stack(x):
        return jnp.concatenate([x, x], axis=0)

    def stack_heads(x):
        return jnp.concatenate([jnp.where(head0, x, 0.0), jnp.where(head0, 0.0, x)], axis=0)

    def halves(x):
        return jnp.where(head0, x[:c], x[c:])

    subs = range(n_sub)
    rows = [slice(i * c, (i + 1) * c) for i in subs]
    each = lambda f, *xs: [f(*a) for a in zip(*xs)]
    r = [r_ref[rw, :] for rw in rows]
    k = [k_ref[rw, :] for rw in rows]
    v = [v_ref[rw, :] for rw in rows]
    lw = [-jnp.exp(-_softplus(-(w0 + wl_ref[rw, :])) - 0.5) for rw in rows]
    a = [_sigmoid(a0 + al_ref[rw, :]) for rw in rows]
    kk = each(lambda k_: k_ * k_k, k)
    nrm = each(lambda x: jnp.sqrt(_dot_exact_rhs(x * x, seg)), kk)
    kk = each(lambda x, n: x / jnp.maximum(n, 1e-12), kk, nrm)
    kmod = each(lambda k_, a_: k_ * (1.0 + (a_ - 1.0) * k_a), k, a)
    al = each(lambda x: -x, kk)
    be = each(lambda x, a_: x * a_, kk, a)
    kmod_s = kmod
    if t_valid < c * n_sub:
        valid = [(bi * (c * n_sub) + i * c + row_id) < t_valid for i in subs]
        mask = lambda x, m: jnp.where(m, x, 0.0)
        lw, al, be, kmod_s = each(mask, lw, valid), each(mask, al, valid), each(mask, be, valid), each(mask, kmod, valid)

    cum = each(lambda x: _dot_exact_lhs(tri, x), lw)
    p_col = each(lambda x: jnp.exp(jnp.transpose(jnp.broadcast_to(x[c - 1:c, :], (LANES, LANES)))), cum)
    p_i = each(lambda x: jnp.exp(-x), cum)
    p_e = each(lambda x: jnp.exp(x[c - 1:c, :] - x), cum)
    a_t = each(lambda al_, cu, lw_: al_ * jnp.exp(cu - lw_), al, cum, lw)
    b_t = each(lambda x, p: x * p, be, p_i)
    k_t = each(lambda x, p: x * p, kmod_s, p_i)
    r_t = each(lambda x, cu: x * jnp.exp(cu), r, cum)
    b_bar = each(lambda x, p: x * p, be, p_e)
    k_bar = each(lambda x, p: x * p, kmod_s, p_e)

    x_a = each(stack_heads, a_t)
    x_r = each(stack_heads, r_t)
    y_b = each(stack, b_t)
    y_k = each(stack, k_t)
    v_st = each(stack, v)
    mm = _dot3 if precise else _dot1
    l_ab = each(lambda x, y: jnp.where(low_s, mm(x, y, _NT), 0.0), x_a, y_b)
    l_ak = each(lambda x, y: jnp.where(low_s, mm(x, y, _NT), 0.0), x_a, y_k)
    l_rb = each(lambda x, y: jnp.where(low_i, mm(x, y, _NT), 0.0), x_r, y_b)
    l_rk = each(lambda x, y: jnp.where(low_i, mm(x, y, _NT), 0.0), x_r, y_k)
    t_inv = each(lambda l: eye_n + l, l_ab)
    pw = l_ab
    span = 2
    while span < c:
        pw = each(lambda p: mm(p, p), pw)
        t_inv = each(lambda t, p: t + mm(t, p), t_inv, pw)
        span *= 2
    resid = each(lambda t, l: (eye_n - t) + _dot3(l, t), t_inv, l_ab)
    t_inv = each(lambda t, e: t + mm(t, e), t_inv, resid)
    a_hat = each(lambda t, x: halves(mm(t, stack(x))), t_inv, a_t)
    lv = each(mm, l_ak, v_st)
    u_hat = each(lambda t, x: halves(mm(t, x)), t_inv, lv)
    r_hat = each(lambda x, l, ah: x + halves(mm(l, stack(ah))), r_t, l_rb, a_hat)
    y_hat = each(lambda lb, uh, lk, vs: halves(mm(lb, stack(uh)) + mm(lk, vs)), l_rb, u_hat, l_rk, v_st)
    g_low = each(lambda b, ah: jnp.where(same_head, mm(b, ah, _TN), 0.0), b_bar, a_hat)
    z_mat = each(lambda b, uh, kb, v_: jnp.where(same_head, mm(b, uh, _TN) + mm(kb, v_, _TN), 0.0),
                 b_bar, u_hat, k_bar, v)

    s = s_scr[...]
    y = []
    for i in subs:
        y.append(mm(r_hat[i], s) + y_hat[i])
        s = s * p_col[i] + mm(g_low[i], s) + z_mat[i]

    inv_hd = 1.0 / hd
    stat = _dot_exact_rhs if precise else _dot_sum16
    mean = each(lambda x: stat(x, seg) * inv_hd, y)
    yc = each(lambda x, m: x - m, y, mean)
    var = each(lambda x: stat(x * x, seg) * inv_hd, yc)
    bonus = each(lambda r_, km, v_: stat(r_ * km * r_k, seg) * v_, r, kmod, v)
    for i in subs:
        yn = yc[i] * lax.rsqrt(var[i] + RW_GN_EPS) * ln_w + ln_b
        y_ref[rows[i], :] = ((yn + bonus[i]) * g_ref[rows[i], :]).astype(y_ref.dtype)

    s_scr[...] = s

    @pl.when(bi == pl.num_programs(2) - 1)
    def _():
        sout_ref[...] = s


def _rw_recur(r, k, v, wl, al, g, par, s0, *, n_seq, tg, t_valid):
    rows, d = r.shape
    c = min(64, tg)
    n_sub = min(16, tg // c)
    tb = c * n_sub
    n_blocks = tg // tb
    n_pairs = d // LANES
    tile = pl.BlockSpec((tb, LANES), lambda b, p, bi: (b * n_blocks + bi, p))
    state = pl.BlockSpec((None, None, LANES, LANES), lambda b, p, bi: (b, p, 0, 0))
    return pl.pallas_call(
        functools.partial(_rw_recur_kernel, chunk=c, n_sub=n_sub, t_valid=t_valid, precise=tg <= SAMPLE_SLOT),
        out_shape=[jax.ShapeDtypeStruct((rows, d), BF16),
                   jax.ShapeDtypeStruct((n_seq, n_pairs, LANES, LANES), F32)],
        grid=(n_seq, n_pairs, n_blocks),
        in_specs=[tile] * 6 + [pl.BlockSpec((8, LANES), lambda b, p, bi: (0, p)), state],
        out_specs=[tile, state],
        scratch_shapes=[pltpu.VMEM((LANES, LANES), F32)],
        compiler_params=_params("parallel", "parallel", "arbitrary"),
        name="rw_recur",
    )(r, k, v, wl, al, g, par, s0)


def _rw_state_to_pairs(wkv):
    b, h, n, _ = wkv.shape
    st = jnp.swapaxes(wkv, -1, -2).reshape(b, h // 2, 2, n, n)
    z = jnp.zeros_like(st[:, :, 0])
    top = jnp.concatenate([st[:, :, 0], z], axis=-1)
    bot = jnp.concatenate([z, st[:, :, 1]], axis=-1)
    return jnp.concatenate([top, bot], axis=-2)


def _rw_pairs_to_state(s):
    b, p, _, _ = s.shape
    n = RW_HEAD_DIM
    h0 = s[:, :, :n, :n]
    h1 = s[:, :, n:, n:]
    st = jnp.stack([h0, h1], axis=2).reshape(b, 2 * p, n, n)
    return jnp.swapaxes(st, -1, -2)


def _rwkv7(h, shift_prev, wkv_prev, w, *, n_seq, tg, t_valid):
    rows, d = h.shape
    h3 = h.reshape(n_seq, tg, d)
    h_prev = jnp.concatenate([shift_prev[:, None, :], h3[:, :-1]], axis=1).reshape(rows, d)
    xr, xw, xk, xv, xa, xg = _rw_mix(h, h_prev, w["rw_mu"])
    r = _matmul(xr, w["rw_w_r"])
    k = _matmul(xk, w["rw_w_k"])
    v = _matmul(xv, w["rw_w_v"])
    wl = _matmul(_matmul(xw, w["rw_w1"], out_dtype=BF16, act="tanh"), w["rw_w2"])
    al = _matmul(_matmul(xa, w["rw_a1"], out_dtype=BF16), w["rw_a2"])
    g = _matmul(_matmul(xg, w["rw_g1"], out_dtype=BF16, act="sigmoid"), w["rw_g2"])
    par = jnp.stack([w["rw_w0"], w["rw_a0"], w["rw_k_k"], w["rw_k_a"], w["rw_r_k"].reshape(-1),
                     w["rw_ln_w"], w["rw_ln_b"], jnp.zeros((d,), F32)])
    y, s_fin = _rw_recur(r, k, v, wl, al, g, par, _rw_state_to_pairs(wkv_prev),
                         n_seq=n_seq, tg=tg, t_valid=t_valid)
    return y, h3[:, t_valid - 1], _rw_pairs_to_state(s_fin)


def _to_heads_kernel(x_ref, o_ref, *, n_heads, hd):
    tm = x_ref.shape[0]
    for h in range(n_heads):
        o_ref[pl.ds(h, tm, stride=n_heads), :] = x_ref[:, h * hd:(h + 1) * hd]


def _to_heads(x, third, n_seq, tg, head_shape):
    rows = x.shape[0]
    hd = head_shape[-1]
    n_heads = math.prod(head_shape[:-1])
    d = n_heads * hd
    tm = min(256, rows)
    out = pl.pallas_call(
        functools.partial(_to_heads_kernel, n_heads=n_heads, hd=hd),
        out_shape=jax.ShapeDtypeStruct((rows * n_heads, hd), x.dtype),
        grid=(rows // tm,),
        in_specs=[pl.BlockSpec((tm, d), lambda i: (i, third))],
        out_specs=pl.BlockSpec((tm * n_heads, hd), lambda i: (i, 0)),
        compiler_params=_params("parallel"),
        name="to_heads",
    )(x)
    return out.reshape((n_seq, tg) + tuple(head_shape))


def _sb_tile(z, mask, later_carry, u_mat):
    lk0 = -_softplus(z)
    lk = lk0 if mask is None else jnp.where(mask, lk0, 0.0)
    later = _dot_sum16(lk, u_mat)
    w = jnp.exp(z + lk0 + later + later_carry)
    if mask is not None:
        w = jnp.where(mask, w, 0.0)
    return w, later_carry + later[:, 0:1] + lk[:, 0:1]


def _suffix_matrix(n):
    j = lax.broadcasted_iota(jnp.int32, (n, n), 0)
    k = lax.broadcasted_iota(jnp.int32, (n, n), 1)
    return jnp.where(j > k, 1.0, 0.0)


def _sb_prompt_kernel(q_ref, k_ref, v_ref, o_ref, *, bq, n_par):
    qi = pl.program_id(2)
    hd = SB_HEAD_DIM
    scale = hd ** -0.5
    heads = range(n_par)
    cols = [slice(h * hd, (h + 1) * hd) for h in heads]
    q = [q_ref[:, c].astype(BF16) for c in cols]
    u_mat = _suffix_matrix(bq)
    row = lax.broadcasted_iota(jnp.int32, (bq, bq), 0)
    col = lax.broadcasted_iota(jnp.int32, (bq, bq), 1)

    def block(kb, state, mask):
        start = pl.multiple_of(kb * bq, bq)
        z = [lax.dot_general(q[h], k_ref[pl.ds(start, bq), cols[h]].astype(BF16), _NT,
                             preferred_element_type=F32) * scale for h in heads]
        tiles = [_sb_tile(z[h], mask, state[h][1], u_mat) for h in heads]
        return tuple(
            (state[h][0] + jnp.dot(tiles[h][0].astype(BF16), v_ref[pl.ds(start, bq), cols[h]].astype(BF16),
                                   preferred_element_type=F32), tiles[h][1]) for h in heads)

    init = tuple((jnp.zeros((bq, hd), F32), jnp.zeros((bq, 1), F32)) for _ in heads)
    state = block(qi, init, col < row)
    state = lax.fori_loop(1, qi + 1, lambda j, st: block(qi - j, st, None), state)
    for h in heads:
        o_ref[:, cols[h]] = state[h][0].astype(o_ref.dtype)


def _sb_prompt(qkv, *, n_seq, tg):
    rows, d3 = qkv.shape
    d = d3 // 3
    hd = SB_HEAD_DIM
    n_heads = d // hd
    n_par = 2 if n_heads % 2 == 0 else 1
    hw = n_par * hd
    n_hb = n_heads // n_par
    bq = min(256, tg)
    nq = tg // bq
    return pl.pallas_call(
        functools.partial(_sb_prompt_kernel, bq=bq, n_par=n_par),
        out_shape=jax.ShapeDtypeStruct((rows, d), BF16),
        grid=(n_seq, n_hb, nq),
        in_specs=[
            pl.BlockSpec((bq, hw), lambda b, h, qi: (b * nq + qi, h)),
            pl.BlockSpec((tg, hw), lambda b, h, qi: (b, n_hb + h)),
            pl.BlockSpec((tg, hw), lambda b, h, qi: (b, 2 * n_hb + h)),
        ],
        out_specs=pl.BlockSpec((bq, hw), lambda b, h, qi: (b * nq + qi, h)),
        compiler_params=_params("parallel", "parallel", "arbitrary"),
        name="sb_prompt",
    )(qkv, qkv, qkv)


def _sb_sample_kernel(pt_ref, q_ref, kn_ref, vn_ref, kc_ref, vc_ref, o_ref, acc_scr, carry_scr,
                      *, n_heads, t_valid, page):
    j = pl.program_id(1)
    hd = SB_HEAD_DIM
    slot = SAMPLE_SLOT
    scale = hd ** -0.5
    n_rows = n_heads * slot

    def tile(get_k, get_v, nk, first):
        q = q_ref[...].astype(BF16)
        z = jnp.concatenate(
            [lax.dot_general(q[:, h * hd:(h + 1) * hd], get_k(h).astype(BF16), _NT, preferred_element_type=F32)
             for h in range(n_heads)], axis=0) * scale
        if first:
            row = lax.broadcasted_iota(jnp.int32, (n_rows, nk), 0)
            col = lax.broadcasted_iota(jnp.int32, (n_rows, nk), 1)
            mask = col < jnp.minimum(row & (slot - 1), t_valid)
            carry = jnp.zeros((n_rows, 1), F32)
        else:
            mask = None
            carry = carry_scr[:, 0:1]
        w, carry = _sb_tile(z, mask, carry, _suffix_matrix(nk))
        carry_scr[...] = jnp.broadcast_to(carry, (n_rows, LANES))
        wb = w.astype(BF16)
        for h in range(n_heads):
            pv = jnp.dot(wb[h * slot:(h + 1) * slot], get_v(h).astype(BF16), preferred_element_type=F32)
            sl = slice(h * hd, (h + 1) * hd)
            acc_scr[:, sl] = pv if first else acc_scr[:, sl] + pv

    @pl.when(j == 0)
    def _():
        tile(lambda h: kn_ref[:, h * hd:(h + 1) * hd], lambda h: vn_ref[:, h * hd:(h + 1) * hd], slot, True)

    @pl.when(j > 0)
    def _():
        tile(lambda h: kc_ref[pl.ds(h, page, stride=n_heads), :],
             lambda h: vc_ref[pl.ds(h, page, stride=n_heads), :], page, False)

    @pl.when(j == pl.num_programs(1) - 1)
    def _():
        o_ref[...] = acc_scr[...].astype(o_ref.dtype)


def _sb_sample(qkv, cache_k, cache_v, page_table, *, t_valid):
    n_seq, n_pages = page_table.shape
    rows, d3 = qkv.shape
    d = d3 // 3
    n_phys, page, n_heads, hd = cache_k.shape
    slot = SAMPLE_SLOT

    ck = cache_k.reshape(n_phys, page * n_heads, hd)
    cv = cache_v.reshape(n_phys, page * n_heads, hd)
    page_spec = pl.BlockSpec((None, page * n_heads, hd), lambda s, j, pt: (pt[s, n_pages - jnp.maximum(j, 1)], 0, 0))

    grid_spec = pltpu.PrefetchScalarGridSpec(
        num_scalar_prefetch=1,
        grid=(n_seq, n_pages + 1),
        in_specs=[
            pl.BlockSpec((slot, d), lambda s, j, pt: (s, 0)),
            pl.BlockSpec((slot, d), lambda s, j, pt: (s, 1)),
            pl.BlockSpec((slot, d), lambda s, j, pt: (s, 2)),
            page_spec,
            page_spec,
        ],
        out_specs=pl.BlockSpec((slot, d), lambda s, j, pt: (s, 0)),
        scratch_shapes=[pltpu.VMEM((slot, d), F32), pltpu.VMEM((n_heads * slot, LANES), F32)],
    )
    return pl.pallas_call(
        functools.partial(_sb_sample_kernel, n_heads=n_heads, t_valid=t_valid, page=page),
        out_shape=jax.ShapeDtypeStruct((rows, d), BF16),
        grid_spec=grid_spec,
        compiler_params=_params("parallel", "arbitrary"),
        name="sb_sample",
    )(page_table, qkv, qkv, qkv, ck, cv)


def _rope_kernel(x_ref, cs_ref, sn_ref, o_ref):
    cs = cs_ref[...]
    sn = sn_ref[...]
    for c in range(x_ref.shape[1] // LANES):
        x = x_ref[:, c * LANES:(c + 1) * LANES]
        o_ref[:, c * LANES:(c + 1) * LANES] = x * cs + pltpu.roll(x, LANES // 2, 1) * sn


def _rope_tables(pos):
    half = DA_HEAD_DIM // 2
    inv_freq = ROPE_THETA ** (-jnp.arange(half, dtype=F32) / half)
    ang = pos.astype(F32)[:, None] * inv_freq[None, :]
    cos, sin = jnp.cos(ang), jnp.sin(ang)
    return jnp.concatenate([cos, cos], axis=1), jnp.concatenate([-sin, sin], axis=1)


def _rope_qk(qkv, pos_rows):
    rows, d3 = qkv.shape
    d2 = d3 // 3 * 2
    cs, sn = _rope_tables(pos_rows)
    tt = cs.shape[0]
    tm = min(256, tt)
    tn = min(1024, d2)
    per = tt // tm
    return pl.pallas_call(
        _rope_kernel,
        out_shape=jax.ShapeDtypeStruct((rows, d2), F32),
        grid=(rows // tm, d2 // tn),
        in_specs=[
            pl.BlockSpec((tm, tn), lambda i, j: (i, j)),
            pl.BlockSpec((tm, LANES), lambda i, j: (i % per, 0)),
            pl.BlockSpec((tm, LANES), lambda i, j: (i % per, 0)),
        ],
        out_specs=pl.BlockSpec((tm, tn), lambda i, j: (i, j)),
        compiler_params=_params("parallel", "parallel"),
        name="rope",
    )(qkv, cs, sn)


def _softmax_tile(s, mask, m_prev, l_prev):
    if mask is not None:
        s = jnp.where(mask, s, -jnp.inf)
    m_new = jnp.maximum(m_prev, jnp.max(s, axis=-1, keepdims=True))
    p = jnp.exp(s - m_new)
    alpha = jnp.exp(m_prev - m_new)
    return p, m_new, alpha, alpha * l_prev + jnp.sum(p, axis=-1, keepdims=True)


def _da_finish(acc1, l1, acc2, l2, lam, subln, lam_init):
    o = acc1 / l1 - lam * (acc2 / l2)
    o = o * lax.rsqrt(jnp.mean(o * o, axis=-1, keepdims=True) + DA_SUBLN_EPS) * subln
    return o * (1.0 - lam_init)


def _da_prompt_kernel(lam_ref, q_ref, k_ref, v_ref, sub_ref, o_ref, *, bq, lam_init):
    qi = pl.program_id(2)
    hd = DA_HEAD_DIM
    scale = hd ** -0.5
    q1 = q_ref[:, :hd].astype(BF16)
    q2 = q_ref[:, hd:].astype(BF16)
    row = lax.broadcasted_iota(jnp.int32, (bq, bq), 0)
    col = lax.broadcasted_iota(jnp.int32, (bq, bq), 1)

    def block(kb, state, mask):
        m1, l1, acc1, m2, l2, acc2 = state
        start = pl.multiple_of(kb * bq, bq)
        k_blk = k_ref[pl.ds(start, bq), :].astype(BF16)
        v_blk = v_ref[pl.ds(start, bq), :].astype(BF16)
        s1 = lax.dot_general(q1, k_blk[:, :hd], _NT, preferred_element_type=F32) * scale
        s2 = lax.dot_general(q2, k_blk[:, hd:], _NT, preferred_element_type=F32) * scale
        p1, m1, a1, l1 = _softmax_tile(s1, mask, m1, l1)
        p2, m2, a2, l2 = _softmax_tile(s2, mask, m2, l2)
        acc1 = a1 * acc1 + jnp.dot(p1.astype(BF16), v_blk, preferred_element_type=F32)
        acc2 = a2 * acc2 + jnp.dot(p2.astype(BF16), v_blk, preferred_element_type=F32)
        return m1, l1, acc1, m2, l2, acc2

    neg = jnp.full((bq, 1), -jnp.inf, F32)
    zero = jnp.zeros((bq, 1), F32)
    zacc = jnp.zeros((bq, 2 * hd), F32)
    state = lax.fori_loop(0, qi, lambda kb, st: block(kb, st, None), (neg, zero, zacc, neg, zero, zacc))
    _, l1, acc1, _, l2, acc2 = block(qi, state, col <= row)
    o_ref[...] = _da_finish(acc1, l1, acc2, l2, lam_ref[0], sub_ref[...], lam_init).astype(o_ref.dtype)


def _da_prompt(qk, qkv, lam, subln, *, n_seq, tg, lam_init):
    rows, d2 = qk.shape
    d = d2 // 2
    hw = 2 * DA_HEAD_DIM
    n_heads = d // hw
    bq = min(256, tg)
    nq = tg // bq
    return pl.pallas_call(
        functools.partial(_da_prompt_kernel, bq=bq, lam_init=lam_init),
        out_shape=jax.ShapeDtypeStruct((rows, d), BF16),
        grid=(n_seq, n_heads, nq),
        in_specs=[
            pl.BlockSpec(memory_space=pltpu.SMEM),
            pl.BlockSpec((bq, hw), lambda b, h, qi: (b * nq + qi, h)),
            pl.BlockSpec((tg, hw), lambda b, h, qi: (b, n_heads + h)),
            pl.BlockSpec((tg, hw), lambda b, h, qi: (b, 2 * n_heads + h)),
            pl.BlockSpec((1, hw), lambda b, h, qi: (0, 0)),
        ],
        out_specs=pl.BlockSpec((bq, hw), lambda b, h, qi: (b * nq + qi, h)),
        compiler_params=_params("parallel", "parallel", "arbitrary"),
        name="da_prompt",
    )(lam, qk, qk, qkv, subln.reshape(1, hw))


def _da_sample_kernel(pt_ref, q_ref, kn_ref, vn_ref, kc_ref, vlo_ref, vhi_ref, lam_ref, sub_ref, o_ref,
                      m_scr, l_scr, acc_scr, *, n_heads, t_valid, page, lam_init):
    j = pl.program_id(1)
    hd = DA_HEAD_DIM
    hw = 2 * hd
    slot = SAMPLE_SLOT
    scale = hd ** -0.5
    n_rows = 2 * n_heads * slot

    def tile(get_k, get_v, nk, first):
        q = q_ref[...].astype(BF16)
        s = jnp.concatenate(
            [lax.dot_general(q[:, i * hd:(i + 1) * hd], get_k(i // 2, i % 2).astype(BF16), _NT,
                             preferred_element_type=F32) for i in range(2 * n_heads)], axis=0) * scale
        if first:
            row = lax.broadcasted_iota(jnp.int32, (n_rows, nk), 0)
            col = lax.broadcasted_iota(jnp.int32, (n_rows, nk), 1)
            mask = col <= jnp.minimum(row & (slot - 1), t_valid - 1)
            m_prev = jnp.full((n_rows, 1), -jnp.inf, F32)
            l_prev = jnp.zeros((n_rows, 1), F32)
        else:
            mask = None
            m_prev = m_scr[:, 0:1]
            l_prev = l_scr[:, 0:1]
        p, m_new, alpha, l_new = _softmax_tile(s, mask, m_prev, l_prev)
        m_scr[...] = jnp.broadcast_to(m_new, (n_rows, LANES))
        l_scr[...] = jnp.broadcast_to(l_new, (n_rows, LANES))
        pb = p.astype(BF16)
        for h in range(n_heads):
            vb = get_v(h).astype(BF16)
            sl = slice(h * hw, (h + 1) * hw)
            for c in range(2):
                rows = slice((2 * h + c) * slot, (2 * h + c + 1) * slot)
                pv = jnp.dot(pb[rows], vb, preferred_element_type=F32)
                acc_scr[c, :, sl] = pv if first else alpha[rows] * acc_scr[c, :, sl] + pv

    @pl.when(j == 0)
    def _():
        tile(lambda h, c: kn_ref[:, (2 * h + c) * hd:(2 * h + c + 1) * hd],
             lambda h: vn_ref[:, h * hw:(h + 1) * hw], slot, True)

    @pl.when(j > 0)
    def _():
        tile(lambda h, c: kc_ref[pl.ds(2 * h + c, page, stride=2 * n_heads), :],
             lambda h: jnp.concatenate([vlo_ref[pl.ds(h, page, stride=n_heads), :],
                                        vhi_ref[pl.ds(h, page, stride=n_heads), :]], axis=1), page, False)

    @pl.when(j == pl.num_programs(1) - 1)
    def _():
        l_all = l_scr[:, 0:1]
        for h in range(n_heads):
            sl = slice(h * hw, (h + 1) * hw)
            r1 = slice(2 * h * slot, (2 * h + 1) * slot)
            r2 = slice((2 * h + 1) * slot, (2 * h + 2) * slot)
            o = _da_finish(acc_scr[0, :, sl], l_all[r1], acc_scr[1, :, sl], l_all[r2],
                           lam_ref[0], sub_ref[...], lam_init)
            o_ref[:, sl] = o.astype(o_ref.dtype)


def _da_sample(qk, qkv, cache_k, cache_v, page_table, lam, subln, *, t_valid, lam_init):
    rows, d2 = qk.shape
    d = d2 // 2
    hd = DA_HEAD_DIM
    hw = 2 * hd
    n_heads = d // hw
    n_seq, n_pages = page_table.shape
    n_phys, page = cache_k.shape[:2]
    ck = cache_k.reshape(n_phys, page * 2 * n_heads, hd)
    cv = cache_v.reshape(n_phys, page * n_heads, hw)
    slot = SAMPLE_SLOT

    def page_map(half):
        return lambda s, j, pt: (pt[s, n_pages - jnp.maximum(j, 1)], 0, half)

    grid_spec = pltpu.PrefetchScalarGridSpec(
        num_scalar_prefetch=1,
        grid=(n_seq, n_pages + 1),
        in_specs=[
            pl.BlockSpec((slot, d), lambda s, j, pt: (s, 0)),
            pl.BlockSpec((slot, d), lambda s, j, pt: (s, 1)),
            pl.BlockSpec((slot, d), lambda s, j, pt: (s, 2)),
            pl.BlockSpec((None, page * 2 * n_heads, hd), page_map(0)),
            pl.BlockSpec((None, page * n_heads, hd), page_map(0)),
            pl.BlockSpec((None, page * n_heads, hd), page_map(1)),
            pl.BlockSpec(memory_space=pltpu.SMEM),
            pl.BlockSpec((1, hw), lambda s, j, pt: (0, 0)),
        ],
        out_specs=pl.BlockSpec((slot, d), lambda s, j, pt: (s, 0)),
        scratch_shapes=[pltpu.VMEM((2 * n_heads * slot, LANES), F32),
                        pltpu.VMEM((2 * n_heads * slot, LANES), F32),
                        pltpu.VMEM((2, slot, d), F32)],
    )
    return pl.pallas_call(
        functools.partial(_da_sample_kernel, n_heads=n_heads, t_valid=t_valid, page=page, lam_init=lam_init),
        out_shape=jax.ShapeDtypeStruct((rows, d), BF16),
        grid_spec=grid_spec,
        compiler_params=_params("parallel", "arbitrary"),
        name="da_sample",
    )(page_table, qk, qk, qkv, ck, cv, cv, lam, subln.reshape(1, hw))


def _gla_kernel(q_ref, k_ref, v_ref, gate_ref, lo_ref, ba_ref, nw_ref, s0_ref, o_ref, sout_ref, s_scr,
                *, tb, c, tg, t_valid, dk):
    ci = pl.program_id(2)

    @pl.when(ci == 0)
    def _():
        s_scr[...] = s0_ref[...]

    ti = lax.broadcasted_iota(jnp.int32, (c, c), 0)
    si = lax.broadcasted_iota(jnp.int32, (c, c), 1)
    tri = jnp.where(ti >= si, 1.0, 0.0)
    row_id = lax.broadcasted_iota(jnp.int32, (c, 1), 0)
    ones_cl = jnp.ones((c, LANES), F32)
    ba = ba_ref[...]
    nw = nw_ref[...]
    for sub in range(tb // c):
        rows = slice(sub * c, (sub + 1) * c)
        q = q_ref[rows, :] * (dk ** -0.5)
        k = k_ref[rows, :]
        v = v_ref[rows, :]
        la = -_softplus(-(lo_ref[rows, :] + ba)) / GLA_TAU
        if t_valid < tg:
            valid = (ci * tb + sub * c + row_id) < t_valid
            la = jnp.where(valid, la, 0.0)
            k = jnp.where(valid, k, 0.0)
        cum = _dot_exact_lhs(tri, la)
        cum_end = cum[c - 1:c, :]
        s = s_scr[...]
        o = _dot1(q * jnp.exp(cum), s)
        att = jnp.zeros((c, c), F32)
        for j in range(c):
            e = jnp.exp(jnp.where(row_id >= j, cum - cum[j:j + 1, :], -jnp.inf))
            p = jnp.sum(q * k[j:j + 1, :] * e, axis=-1, keepdims=True)
            att = jnp.where(si == j, p, att)
        o = o + _dot1(att, v)
        dec = jnp.exp(_dot_exact_rhs(la, ones_cl, _TN)[:, 0:1])
        s_scr[...] = s * dec + _dot1(k * jnp.exp(cum_end - cum), v, _TN)
        on = o * lax.rsqrt(jnp.mean(o * o, axis=-1, keepdims=True) + GLA_NORM_EPS) * nw
        g = gate_ref[rows, :]
        o_ref[rows, :] = (on * (g * _sigmoid(g))).astype(o_ref.dtype)

    @pl.when(ci == pl.num_programs(2) - 1)
    def _():
        sout_ref[...] = s_scr[...]


def _gla(proj, lo, ba, norm_w, s0, *, n_seq, tg, t_valid):
    rows = proj.shape[0]
    key = lo.shape[1]
    d = (proj.shape[1] - 2 * key) // 2
    nh = GLA_HEADS
    dk, dv = key // nh, d // nh
    tb = min(128, tg)
    c = min(32, tb)
    nc = tg // tb
    kv = key // dv
    return pl.pallas_call(
        functools.partial(_gla_kernel, tb=tb, c=c, tg=tg, t_valid=t_valid, dk=dk),
        out_shape=[jax.ShapeDtypeStruct((rows, d), BF16), jax.ShapeDtypeStruct((n_seq, nh, dk, dv), F32)],
        grid=(n_seq, nh, nc),
        in_specs=[
            pl.BlockSpec((tb, dk), lambda b, h, ci: (b * nc + ci, h)),
            pl.BlockSpec((tb, dk), lambda b, h, ci: (b * nc + ci, nh + h)),
            pl.BlockSpec((tb, dv), lambda b, h, ci: (b * nc + ci, 2 * kv + h)),
            pl.BlockSpec((tb, dv), lambda b, h, ci: (b * nc + ci, 2 * kv + nh + h)),
            pl.BlockSpec((tb, dk), lambda b, h, ci: (b * nc + ci, h)),
            pl.BlockSpec((1, dk), lambda b, h, ci: (0, h)),
            pl.BlockSpec((1, dv), lambda b, h, ci: (0, 0)),
            pl.BlockSpec((None, None, dk, dv), lambda b, h, ci: (b, h, 0, 0)),
        ],
        out_specs=[pl.BlockSpec((tb, dv), lambda b, h, ci: (b * nc + ci, h)),
                   pl.BlockSpec((None, None, dk, dv), lambda b, h, ci: (b, h, 0, 0))],
        scratch_shapes=[pltpu.VMEM((dk, dv), F32)],
        compiler_params=_params("parallel", "parallel", "arbitrary"),
        name="gla",
    )(proj, proj, proj, proj, lo, ba.reshape(1, key), norm_w.reshape(1, dv), s0)


MOE_BLOCK_ROWS = 128


def _moe_route(logits, n_groups, n_experts):
    g_logits = logits[:, :n_groups]
    grp = jnp.argmax(g_logits, axis=-1)
    g_w = jnp.take_along_axis(jax.nn.softmax(g_logits, axis=-1), grp[:, None], axis=-1)
    e_logits = logits[:, n_groups:n_groups + n_groups * n_experts].reshape(-1, n_groups, n_experts)
    e_logits = jnp.take_along_axis(e_logits, grp[:, None, None], axis=1)[:, 0]
    top_val, top_idx = lax.top_k(e_logits, MOE_TOPK)
    gate = g_w * jax.nn.softmax(top_val, axis=-1)
    return (grp[:, None] * n_experts + top_idx).astype(jnp.int32), gate


def _moe_plan(expert_id, n_exp):
    tm = MOE_BLOCK_ROWS
    n_tok, top_k = expert_id.shape
    n_assign = n_tok * top_k
    e_flat = expert_id.reshape(-1)
    order = jnp.argsort(e_flat)
    inv = jnp.argsort(order)
    counts = jnp.bincount(e_flat, length=n_exp)
    start = jnp.cumsum(counts) - counts
    padded = (counts + tm - 1) // tm * tm
    pad_end = jnp.cumsum(padded)
    pad_start = pad_end - padded
    pos = (inv + (pad_start - start)[e_flat]).astype(jnp.int32).reshape(n_tok, top_k)
    n_blocks = -(-n_assign // tm) + n_exp
    first_row = jnp.arange(n_blocks) * tm
    block_expert = jnp.minimum(jnp.sum(pad_end[None, :] <= first_row[:, None], axis=1), n_exp - 1)
    row = jnp.arange(n_blocks * tm)
    e_row = jnp.repeat(block_expert, tm)
    rank = row - pad_start[e_row]
    src = order[jnp.clip(rank + start[e_row], 0, n_assign - 1)] // top_k
    src_tok = jnp.where(rank < counts[e_row], src, 0).astype(jnp.int32)
    n_used = (pad_end[-1] // tm).astype(jnp.int32).reshape(1)
    return src_tok, pos, block_expert.astype(jnp.int32), n_used


def _new_expert(be_ref, b):
    return jnp.logical_or(b == 0, be_ref[b] != be_ref[jnp.maximum(b - 1, 0)])


def _moe_up_kernel(be_ref, nu_ref, x_ref, wg_ref, wu_ref, h_ref, wg_bf, wu_bf):
    b = pl.program_id(0)
    used = b < nu_ref[0]

    @pl.when(jnp.logical_and(used, _new_expert(be_ref, b)))
    def _():
        wg_bf[...] = wg_ref[...].astype(BF16)
        wu_bf[...] = wu_ref[...].astype(BF16)

    @pl.when(used)
    def _():
        x = x_ref[...].astype(BF16)
        g = jnp.dot(x, wg_bf[...], preferred_element_type=F32)
        u = jnp.dot(x, wu_bf[...], preferred_element_type=F32)
        h_ref[...] = (g * _sigmoid(g) * u).astype(h_ref.dtype)

    @pl.when(jnp.logical_not(used))
    def _():
        h_ref[...] = jnp.zeros_like(h_ref)


def _moe_down_kernel(be_ref, nu_ref, h_ref, wd_ref, o_ref, wd_bf):
    b = pl.program_id(0)
    used = b < nu_ref[0]

    @pl.when(jnp.logical_and(used, _new_expert(be_ref, b)))
    def _():
        wd_bf[...] = wd_ref[...].astype(BF16)

    @pl.when(used)
    def _():
        o_ref[...] = jnp.dot(h_ref[...], wd_bf[...], preferred_element_type=F32)

    @pl.when(jnp.logical_not(used))
    def _():
        o_ref[...] = jnp.zeros_like(o_ref)


def _moe_experts(rows, block_expert, n_used, w_gate, w_up, w_down, layer):
    tm = MOE_BLOCK_ROWS
    n_rows, d = rows.shape
    f = w_gate.shape[3]
    n_blocks = n_rows // tm
    up_spec = pltpu.PrefetchScalarGridSpec(
        num_scalar_prefetch=2,
        grid=(n_blocks,),
        in_specs=[
            pl.BlockSpec((tm, d), lambda b, be, nu: (b, 0)),
            pl.BlockSpec((None, None, d, f), lambda b, be, nu: (layer, be[b], 0, 0)),
            pl.BlockSpec((None, None, d, f), lambda b, be, nu: (layer, be[b], 0, 0)),
        ],
        out_specs=pl.BlockSpec((tm, f), lambda b, be, nu: (b, 0)),
        scratch_shapes=[pltpu.VMEM((d, f), BF16), pltpu.VMEM((d, f), BF16)],
    )
    h = pl.pallas_call(
        _moe_up_kernel,
        out_shape=jax.ShapeDtypeStruct((n_rows, f), BF16),
        grid_spec=up_spec,
        compiler_params=_params("arbitrary"),
        name="moe_up",
    )(block_expert, n_used, rows, w_gate, w_up)
    down_spec = pltpu.PrefetchScalarGridSpec(
        num_scalar_prefetch=2,
        grid=(n_blocks,),
        in_specs=[
            pl.BlockSpec((tm, f), lambda b, be, nu: (b, 0)),
            pl.BlockSpec((None, None, f, d), lambda b, be, nu: (layer, be[b], 0, 0)),
        ],
        out_specs=pl.BlockSpec((tm, d), lambda b, be, nu: (b, 0)),
        scratch_shapes=[pltpu.VMEM((f, d), BF16)],
    )
    return pl.pallas_call(
        _moe_down_kernel,
        out_shape=jax.ShapeDtypeStruct((n_rows, d), F32),
        grid_spec=down_spec,
        compiler_params=_params("arbitrary"),
        name="moe_down",
    )(block_expert, n_used, h, w_down)


def _moe_combine_kernel(x_ref, o0_ref, o1_ref, g_ref, m_ref, y_ref):
    g = g_ref[...]
    y = o0_ref[...] * g[:, 0:1] + o1_ref[...] * g[:, 1:2]
    y_ref[...] = x_ref[...] + m_ref[...] * y


def _moe_combine(x, o0, o1, gates, gmod, *, tg):
    rows, d = x.shape
    tm = min(256, tg) if gmod.shape[1] == 1 else rows
    tile = pl.BlockSpec((tm, d), lambda i: (i, 0))
    return pl.pallas_call(
        _moe_combine_kernel,
        out_shape=jax.ShapeDtypeStruct((rows, d), F32),
        grid=(rows // tm,),
        in_specs=[tile, tile, tile, pl.BlockSpec((tm, MOE_TOPK), lambda i: (i, 0)), _mod_spec(gmod, tm, tg, d)],
        out_specs=tile,
        compiler_params=_params("parallel"),
        name="moe_combine",
    )(x, o0, o1, gates, gmod)


def _moe_layer(groups, xs, hs, logits, gmods, n_groups, w_gate, w_up, w_down, layer):
    n_experts = w_gate.shape[1] // n_groups
    d = xs[0].shape[1]
    real_h, real_l = [], []
    for grp, h, lg in zip(groups, hs, logits):
        n_seq, tg, tv = grp
        real_h.append(h.reshape(n_seq, tg, d)[:, :tv].reshape(n_seq * tv, d))
        real_l.append(lg.reshape(n_seq, tg, -1)[:, :tv].reshape(n_seq * tv, -1))
    h_all = jnp.concatenate(real_h, axis=0)
    expert_id, gate = _moe_route(jnp.concatenate(real_l, axis=0), n_groups, n_experts)
    src_tok, pos, block_expert, n_used = _moe_plan(expert_id, n_groups * n_experts)
    out = _moe_experts(h_all[src_tok], block_expert, n_used, w_gate, w_up, w_down, layer)
    new_xs = []
    start = 0
    for grp, x, gmod in zip(groups, xs, gmods):
        n_seq, tg, tv = grp
        n = n_seq * tv
        rows = pos[start:start + n]
        parts = []
        for a in (out[rows[:, 0]], out[rows[:, 1]], gate[start:start + n]):
            a = a.reshape(n_seq, tv, -1)
            if tv < tg:
                a = jnp.pad(a, ((0, 0), (0, tg - tv), (0, 0)))
            parts.append(a.reshape(n_seq * tg, -1))
        new_xs.append(_moe_combine(x, parts[0], parts[1], parts[2], gmod, tg=tg))
        start += n
    return new_xs


def kernel(x_prompt, x_sample, c_prompt, c_sample, state_rwkv_shift, state_rwkv_wkv, cache_sb_k, cache_sb_v, cache_da_k, cache_da_v, state_gla, page_table, ada_w, ada_b, norm_mix, norm_ffn, norm_final, rw_mu, rw_w_r, rw_w_k, rw_w_v, rw_w_o, rw_w0, rw_w1, rw_w2, rw_a0, rw_a1, rw_a2, rw_g1, rw_g2, rw_k_k, rw_k_a, rw_r_k, rw_ln_w, rw_ln_b, sb_w_in, sb_w_out, da_w_in, da_w_out, da_lambda, da_subln, gla_w_in, gla_wa1, gla_wa2, gla_ba, gla_norm, gla_w_out, moe_w_group, moe_w_expert, moe_w_gate, moe_w_up, moe_w_down):
    n_p, t_p, d = x_prompt.shape
    n_s, t_s, _ = x_sample.shape
    depth = ada_w.shape[0]
    slot = SAMPLE_SLOT
    n_pages, page = page_table.shape[1], cache_sb_k.shape[1]
    past_len = n_pages * page
    groups = [(n_p, t_p, t_p), (n_s, slot, t_s)]

    n_c = n_p + n_s
    c_all = jnp.concatenate([c_prompt, c_sample], axis=0)
    c_all = jnp.pad(c_all, ((0, -n_c % SUBLANES), (0, 0)))
    mod = _adaln(c_all, ada_w, ada_b)

    xs = [x_prompt.reshape(n_p * t_p, d),
          jnp.pad(x_sample, ((0, 0), (0, slot - t_s), (0, 0))).reshape(n_s * slot, d)]
    rw = dict(rw_mu=rw_mu, rw_w0=rw_w0, rw_a0=rw_a0, rw_k_k=rw_k_k, rw_k_a=rw_k_a, rw_r_k=rw_r_k,
              rw_ln_w=rw_ln_w, rw_ln_b=rw_ln_b)
    for name, val in (("rw_w_r", rw_w_r), ("rw_w_k", rw_w_k), ("rw_w_v", rw_w_v), ("rw_w1", rw_w1),
                      ("rw_w2", rw_w2), ("rw_a1", rw_a1), ("rw_a2", rw_a2), ("rw_g1", rw_g1), ("rw_g2", rw_g2)):
        rw[name] = val.astype(BF16)
    pos = [jnp.arange(t_p, dtype=jnp.int32), jnp.tile(past_len + jnp.arange(slot, dtype=jnp.int32), n_s)]
    rw_state = [(jnp.zeros((n_p, d), F32), jnp.zeros((n_p, d // RW_HEAD_DIM, RW_HEAD_DIM, RW_HEAD_DIM), F32)),
                (state_rwkv_shift, state_rwkv_wkv)]
    gla_state = [jnp.zeros((n_p,) + state_gla.shape[1:], F32), state_gla]
    outs = [dict(), dict()]

    for i in range(depth):
        mods = []
        for gi, (n_seq, tg, tv) in enumerate(groups):
            rows = mod[i, :n_p] if gi == 0 else mod[i, n_p:n_c]
            six = [rows[:, j * d:(j + 1) * d] for j in range(6)]
            if gi == 0:
                mods.append([m[:, None, :] for m in six])
            else:
                mods.append([jnp.repeat(m, tg, axis=0)[None] for m in six])
        kind = i % 4
        for gi, (n_seq, tg, tv) in enumerate(groups):
            m = mods[gi]
            x = xs[gi]
            o = outs[gi]

            def rows_of(a, width_shape):
                return a.reshape((n_seq, tg) + width_shape)[:, :tv]

            if kind == 0:
                (h,) = _norm_mod(x, norm_mix[i], m[0], m[1], tg=tg, out_dtypes=(F32,))
                y, o["rw_shift"], o["rw_wkv"] = _rwkv7(h, rw_state[gi][0], rw_state[gi][1], rw,
                                                       n_seq=n_seq, tg=tg, t_valid=tv)
                w_out = rw_w_o
            elif kind == 1:
                (h,) = _norm_mod(x, norm_mix[i], m[0], m[1], tg=tg, out_dtypes=(BF16,))
                qkv = _matmul(h, sb_w_in.astype(BF16))
                n_h = d // SB_HEAD_DIM
                o["sb_k"] = _to_heads(qkv, 1, n_seq, tg, (n_h, SB_HEAD_DIM))[:, :tv]
                o["sb_v"] = _to_heads(qkv, 2, n_seq, tg, (n_h, SB_HEAD_DIM))[:, :tv]
                if gi == 0:
                    y = _sb_prompt(qkv, n_seq=n_seq, tg=tg)
                else:
                    y = _sb_sample(qkv, cache_sb_k, cache_sb_v, page_table, t_valid=tv)
                w_out = sb_w_out
            elif kind == 2:
                lam_init = 0.8 - 0.6 * math.exp(-0.3 * i)
                lf = da_lambda.astype(F32)
                lam = (jnp.exp(jnp.sum(lf[0] * lf[1])) - jnp.exp(jnp.sum(lf[2] * lf[3])) + lam_init).reshape(1)
                (h,) = _norm_mod(x, norm_mix[i], m[0], m[1], tg=tg, out_dtypes=(BF16,))
                qkv = _matmul(h, da_w_in.astype(BF16))
                qk = _rope_qk(qkv, pos[gi])
                n_h = d // (2 * DA_HEAD_DIM)
                o["da_k"] = _to_heads(qk, 1, n_seq, tg, (n_h, 2, DA_HEAD_DIM))[:, :tv]
                o["da_v"] = rows_of(qkv[:, 2 * d:], (n_h, 2 * DA_HEAD_DIM))
                if gi == 0:
                    y = _da_prompt(qk, qkv, lam, da_subln, n_seq=n_seq, tg=tg, lam_init=lam_init)
                else:
                    y = _da_sample(qk, qkv, cache_da_k, cache_da_v, page_table, lam, da_subln,
                                   t_valid=tv, lam_init=lam_init)
                w_out = da_w_out
            else:
                (h,) = _norm_mod(x, norm_mix[i], m[0], m[1], tg=tg, out_dtypes=(BF16,))
                proj = _matmul(h, gla_w_in.astype(BF16))
                lo = _matmul(_matmul(h, gla_wa1.astype(BF16), out_dtype=BF16), gla_wa2.astype(BF16))
                y, o["gla"] = _gla(proj, lo, gla_ba, gla_norm, gla_state[gi], n_seq=n_seq, tg=tg, t_valid=tv)
                w_out = gla_w_out
            xs[gi] = _matmul(y, w_out.astype(BF16), resid=x, gate=m[2], tg=tg)

        n_g, n_e = moe_w_group.shape[2], moe_w_expert.shape[3]
        router_w = jnp.concatenate([moe_w_group[i]] + [moe_w_expert[i, g] for g in range(n_g)], axis=1)
        router_w = jnp.pad(router_w, ((0, 0), (0, -router_w.shape[1] % LANES)))
        hs, logits = [], []
        for gi, (n_seq, tg, tv) in enumerate(groups):
            m = mods[gi]
            h, lg = _norm_mod(xs[gi], norm_ffn[i], m[3], m[4], tg=tg, out_dtypes=(F32,), router_w=router_w)
            hs.append(h)
            logits.append(lg)
        xs = _moe_layer(groups, xs, hs, logits, [mods[0][5], mods[1][5]], n_g,
                        moe_w_gate, moe_w_up, moe_w_down, i)

    ys = []
    for gi, (n_seq, tg, tv) in enumerate(groups):
        (y,) = _norm_mod(xs[gi], norm_final, None, None, tg=tg, out_dtypes=(F32,))
        ys.append(y.reshape(n_seq, tg, d)[:, :tv])
    names = ("rw_shift", "rw_wkv", "sb_k", "sb_v", "da_k", "da_v", "gla")
    return (ys[0], ys[1]) + tuple(outs[0][n] for n in names) + tuple(outs[1][n] for n in names)
```
